```python
import math
import jax, jax.numpy as jnp
from jax import lax
import numpy as np

D_MODEL = 4096
BATCH = 4
SEQ = 4096
DEPTH = 1

HEAD_DIM = 128
N_HEADS_A = 16
N_KV_A = 2
WINDOW = 128
N_HEADS_B = 16
N_KV_B = 4
MOBA_BLOCK = 256
MOBA_TOPK = 3
MOBA_Q_CHUNK = 16
D_FF = 11008
EPS = 1e-6
NEG_INF = -1e30
N_MOD = 9

WIDTH_A = N_HEADS_A * HEAD_DIM
WIDTH_B = N_HEADS_B * HEAD_DIM
KV_A = N_KV_A * HEAD_DIM
KV_B = N_KV_B * HEAD_DIM
D_MIX = WIDTH_A + WIDTH_B
D_IN = WIDTH_A + 2 * KV_A + WIDTH_B + 2 * KV_B

kernel_name = "hybrid_swa_sink_moba_macaron_block"


def rms_norm(x, g):
    xf = x.astype(jnp.float32)
    y = xf * lax.rsqrt(jnp.mean(xf * xf, axis=-1, keepdims=True) + EPS)
    return (y * g.astype(jnp.float32)).astype(x.dtype)


def alibi_slopes(n):
    return 2.0 ** (-8.0 * jnp.arange(1, n + 1, dtype=jnp.float32) / n)


def swiglu(h, w_gate, w_up, w_down):
    return (jax.nn.silu(h @ w_gate) * (h @ w_up)) @ w_down


def sliding_window_attention(q, k, v, sinks, slopes):
    b, s = q.shape[0], q.shape[1]
    nb = s // WINDOW
    g = N_HEADS_A // N_KV_A
    qb = q.reshape(b, nb, WINDOW, N_KV_A, g, HEAD_DIM)
    kb = k.reshape(b, nb, WINDOW, N_KV_A, HEAD_DIM)
    vb = v.reshape(b, nb, WINDOW, N_KV_A, HEAD_DIM)

    def with_prev(t):
        prev = jnp.pad(t, ((0, 0), (1, 0), (0, 0), (0, 0), (0, 0)))[:, :-1]
        return jnp.concatenate([prev, t], axis=2)

    kk, vv = with_prev(kb), with_prev(vb)
    scores = jnp.einsum('bnqhgd,bnkhd->bnhgqk', qb, kk).astype(jnp.float32) * (HEAD_DIM ** -0.5)
    qi = jnp.arange(WINDOW)[:, None] + WINDOW
    kj = jnp.arange(2 * WINDOW)[None, :]
    dist = qi - kj
    valid = (dist >= 0) & (dist < WINDOW)
    valid = valid[None] & ((jnp.arange(nb)[:, None, None] > 0) | (kj[None] >= WINDOW))
    m = slopes.reshape(N_KV_A, g)[:, :, None, None]
    scores = scores - m * dist.astype(jnp.float32)
    scores = jnp.where(valid[None, :, None, None], scores, NEG_INF)
    sink = sinks.astype(jnp.float32).reshape(N_KV_A, g)[:, :, None, None]
    mx = jnp.maximum(scores.max(axis=-1, keepdims=True), sink)
    p = jnp.exp(scores - mx)
    denom = p.sum(axis=-1, keepdims=True) + jnp.exp(sink - mx)
    o = jnp.einsum('bnhgqk,bnkhd->bnqhgd', (p / denom).astype(v.dtype), vv)
    return o.reshape(b, s, WIDTH_A)


def moba_attention(q, k, v, slopes):
    b, s = q.shape[0], q.shape[1]
    g = N_HEADS_B // N_KV_B
    s_pad = -(-s // MOBA_BLOCK) * MOBA_BLOCK
    pad = ((0, 0), (0, s_pad - s), (0, 0), (0, 0))
    q = jnp.pad(q, pad).reshape(b, s_pad, N_KV_B, g, HEAD_DIM)
    k = jnp.pad(k, pad)
    v = jnp.pad(v, pad)
    nblk = s_pad // MOBA_BLOCK
    top_k = min(MOBA_TOPK, nblk)
    kb = k.reshape(b, nblk, MOBA_BLOCK, N_KV_B, HEAD_DIM).transpose(0, 3, 1, 2, 4)
    vb = v.reshape(b, nblk, MOBA_BLOCK, N_KV_B, HEAD_DIM).transpose(0, 3, 1, 2, 4)
    k_mean = kb.astype(jnp.float32).mean(axis=3)
    gate = jnp.einsum('bthgd,bhnd->bthgn', q.astype(jnp.float32), k_mean)
    own = jnp.arange(s_pad) // MOBA_BLOCK
    past = jnp.arange(nblk)[None, :] < own[:, None]
    gate = jnp.where(past[None, :, None, None, :], gate, NEG_INF)
    _, sel = lax.top_k(gate, top_k)
    scale = HEAD_DIM ** -0.5
    m = slopes.reshape(N_KV_B, g)
    b_ix = jnp.arange(b)[:, None, None, None, None]
    h_ix = jnp.arange(N_KV_B)[None, None, :, None, None]
    blk_pos = jnp.arange(MOBA_BLOCK)
    n_chunks = s_pad // MOBA_Q_CHUNK

    def chunk(ci):
        start = ci * MOBA_Q_CHUNK
        qc = lax.dynamic_slice_in_dim(q, start, MOBA_Q_CHUNK, axis=1)
        sc = lax.dynamic_slice_in_dim(sel, start, MOBA_Q_CHUNK, axis=1)
        tq = start + jnp.arange(MOBA_Q_CHUNK)
        own_blk = start // MOBA_BLOCK
        k_sel = kb[b_ix, h_ix, sc]
        v_sel = vb[b_ix, h_ix, sc]
        s_sel = jnp.einsum('bchgd,bchgnkd->bchgnk', qc, k_sel).astype(jnp.float32) * scale
        key_pos = sc[..., None] * MOBA_BLOCK + blk_pos
        dist_sel = (tq[None, :, None, None, None, None] - key_pos).astype(jnp.float32)
        valid_sel = (sc < own_blk)[..., None]
        s_sel = jnp.where(valid_sel, s_sel - m[:, :, None, None] * dist_sel, NEG_INF)
        s_sel = s_sel.reshape(b, MOBA_Q_CHUNK, N_KV_B, g, top_k * MOBA_BLOCK)
        k_own = lax.dynamic_index_in_dim(kb, own_blk, axis=2, keepdims=False)
        v_own = lax.dynamic_index_in_dim(vb, own_blk, axis=2, keepdims=False)
        s_own = jnp.einsum('bchgd,bhkd->bchgk', qc, k_own).astype(jnp.float32) * scale
        dist_own = tq[:, None] - (own_blk * MOBA_BLOCK + blk_pos)[None, :]
        d_own = dist_own[None, :, None, None, :]
        s_own = jnp.where(d_own >= 0, s_own - m[None, None, :, :, None] * d_own.astype(jnp.float32), NEG_INF)
        p = jax.nn.softmax(jnp.concatenate([s_sel, s_own], axis=-1), axis=-1)
        p_sel = p[..., :top_k * MOBA_BLOCK].reshape(b, MOBA_Q_CHUNK, N_KV_B, g, top_k, MOBA_BLOCK).astype(v.dtype)
        p_own = p[..., top_k * MOBA_BLOCK:].astype(v.dtype)
        return (jnp.einsum('bchgnk,bchgnkd->bchgd', p_sel, v_sel)
                + jnp.einsum('bchgk,bhkd->bchgd', p_own, v_own))

    o = lax.map(chunk, jnp.arange(n_chunks))
    o = o.transpose(1, 0, 2, 3, 4, 5).reshape(b, s_pad, WIDTH_B)
    return o[:, :s]


def hybrid_mixer(h, w_in, sinks, norm_a, norm_b, w_out, slopes_a, slopes_b):
    b, s = h.shape[0], h.shape[1]
    proj = h @ w_in
    o0 = 0
    qa = proj[..., o0:o0 + WIDTH_A].reshape(b, s, N_HEADS_A, HEAD_DIM); o0 += WIDTH_A
    ka = proj[..., o0:o0 + KV_A].reshape(b, s, N_KV_A, HEAD_DIM); o0 += KV_A
    va = proj[..., o0:o0 + KV_A].reshape(b, s, N_KV_A, HEAD_DIM); o0 += KV_A
    qb = proj[..., o0:o0 + WIDTH_B].reshape(b, s, N_HEADS_B, HEAD_DIM); o0 += WIDTH_B
    kb = proj[..., o0:o0 + KV_B].reshape(b, s, N_KV_B, HEAD_DIM); o0 += KV_B
    vb = proj[..., o0:o0 + KV_B].reshape(b, s, N_KV_B, HEAD_DIM)
    ya = rms_norm(sliding_window_attention(qa, ka, va, sinks, slopes_a), norm_a)
    yb = rms_norm(moba_attention(qb, kb, vb, slopes_b), norm_b)
    return jnp.concatenate([ya, yb], axis=-1) @ w_out


def setup_inputs(seed: int = 0) -> dict:
    key = jax.random.key(seed)
    ks = jax.random.split(key, 24)
    f32 = jnp.float32

    def nrm(k, shape, scale):
        return jax.random.normal(k, shape, f32) * scale

    def gain(k, n):
        return 1.0 + 0.05 * jax.random.normal(k, (DEPTH, n), f32)

    return {
        "x": nrm(ks[0], (BATCH, SEQ, D_MODEL), 1.0),
        "c": nrm(ks[1], (BATCH, D_MODEL), 1.0),
        "w_ada": nrm(ks[2], (DEPTH, D_MODEL, N_MOD * D_MODEL), D_MODEL ** -0.5),
        "b_ada": nrm(ks[3], (DEPTH, N_MOD * D_MODEL), 0.01),
        "ffn1_norm_pre": gain(ks[4], D_MODEL),
        "ffn1_norm_post": gain(ks[5], D_MODEL),
        "ffn1_w_gate": nrm(ks[6], (DEPTH, D_MODEL, D_FF), D_MODEL ** -0.5),
        "ffn1_w_up": nrm(ks[7], (DEPTH, D_MODEL, D_FF), D_MODEL ** -0.5),
        "ffn1_w_down": nrm(ks[8], (DEPTH, D_FF, D_MODEL), D_FF ** -0.5),
        "mix_norm_pre": gain(ks[9], D_MODEL),
        "mix_norm_post": gain(ks[10], D_MODEL),
        "w_in": nrm(ks[11], (DEPTH, D_MODEL, D_IN), D_MODEL ** -0.5),
        "attn_sinks": nrm(ks[12], (DEPTH, N_HEADS_A), 0.5),
        "norm_a": gain(ks[13], WIDTH_A),
        "norm_b": gain(ks[14], WIDTH_B),
        "w_out": nrm(ks[15], (DEPTH, D_MIX, D_MODEL), D_MIX ** -0.5),
        "ffn2_norm_pre": gain(ks[16], D_MODEL),
        "ffn2_norm_post": gain(ks[17], D_MODEL),
        "ffn2_w_gate": nrm(ks[18], (DEPTH, D_MODEL, D_FF), D_MODEL ** -0.5),
        "ffn2_w_up": nrm(ks[19], (DEPTH, D_MODEL, D_FF), D_MODEL ** -0.5),
        "ffn2_w_down": nrm(ks[20], (DEPTH, D_FF, D_MODEL), D_FF ** -0.5),
    }


def reference(x, c, w_ada, b_ada, ffn1_norm_pre, ffn1_norm_post, ffn1_w_gate, ffn1_w_up, ffn1_w_down,
              mix_norm_pre, mix_norm_post, w_in, attn_sinks, norm_a, norm_b, w_out,
              ffn2_norm_pre, ffn2_norm_post, ffn2_w_gate, ffn2_w_up, ffn2_w_down):
    b = x.shape[0]
    slopes_a = alibi_slopes(N_HEADS_A)
    slopes_b = alibi_slopes(N_HEADS_B)
    for l in range(DEPTH):
        mod = (jax.nn.silu(c) @ w_ada[l] + b_ada[l]).reshape(b, N_MOD, D_MODEL)[:, :, None, :]

        def sublayer(x, fn, i, g_pre, g_post, res_w):
            shift, scale, gate = mod[:, 3 * i], mod[:, 3 * i + 1], mod[:, 3 * i + 2]
            h = rms_norm(x, g_pre) * (1.0 + scale) + shift
            return x + res_w * gate * rms_norm(fn(h), g_post)

        x = sublayer(x, lambda h: swiglu(h, ffn1_w_gate[l], ffn1_w_up[l], ffn1_w_down[l]),
                     0, ffn1_norm_pre[l], ffn1_norm_post[l], 0.5)
        x = sublayer(x, lambda h: hybrid_mixer(h, w_in[l], attn_sinks[l], norm_a[l], norm_b[l], w_out[l],
                                               slopes_a, slopes_b),
                     1, mix_norm_pre[l], mix_norm_post[l], 1.0)
        x = sublayer(x, lambda h: swiglu(h, ffn2_w_gate[l], ffn2_w_up[l], ffn2_w_down[l]),
                     2, ffn2_norm_pre[l], ffn2_norm_post[l], 0.5)
    return x
```

```python
import functools

import jax
import jax.numpy as jnp
from jax import lax
from jax.experimental import pallas as pl
from jax.experimental.pallas import tpu as pltpu

D_MODEL = 4096
HEAD_DIM = 128
N_HEADS_A = 16
N_KV_A = 2
WINDOW = 128
N_HEADS_B = 16
N_KV_B = 4
MOBA_BLOCK = 256
MOBA_TOPK = 3
D_FF = 11008
EPS = 1e-6
NEG_INF = -1e30
N_MOD = 9

WIDTH_A = N_HEADS_A * HEAD_DIM
WIDTH_B = N_HEADS_B * HEAD_DIM
KV_A = N_KV_A * HEAD_DIM
KV_B = N_KV_B * HEAD_DIM
D_IN = WIDTH_A + 2 * KV_A + WIDTH_B + 2 * KV_B

OFF_QA = 0
OFF_QB = OFF_QA + WIDTH_A
OFF_KA = OFF_QB + WIDTH_B
OFF_VA = OFF_KA + KV_A
OFF_KB = OFF_VA + KV_A
OFF_VB = OFF_KB + KV_B

D_FF_PAD = 11264

MIB = 1024 * 1024
F32 = jnp.float32
BF16 = jnp.bfloat16
NT_DIMS = (((1,), (1,)), ((), ()))


def _params(n_grid, vmem_mib):
    return pltpu.CompilerParams(dimension_semantics=("arbitrary",) * n_grid,
                                vmem_limit_bytes=vmem_mib * MIB)


def _rms(v, g):
    return v * lax.rsqrt(jnp.mean(v * v, axis=-1, keepdims=True) + EPS) * g


def _ada_kernel(c_ref, w_ref, b_ref, o_ref):
    c = c_ref[...]
    s = c * jax.nn.sigmoid(c)
    hi = s.astype(BF16)
    lo = (s - hi.astype(F32)).astype(BF16)
    lhs = jnp.concatenate([hi, lo], axis=0)
    r = jnp.dot(lhs, w_ref[...].astype(BF16), preferred_element_type=F32)
    o_ref[...] = r[:8] + r[8:] + b_ref[...]


def _ada(c, w, b):
    bsz, d = c.shape
    n = w.shape[1]
    tn = 1024
    c8 = jnp.pad(c, ((0, 8 - bsz), (0, 0)))
    out = pl.pallas_call(
        _ada_kernel,
        grid=(n // tn,),
        in_specs=[pl.BlockSpec((8, d), lambda j: (0, 0)),
                  pl.BlockSpec((d, tn), lambda j: (0, j)),
                  pl.BlockSpec((1, tn), lambda j: (0, j))],
        out_specs=pl.BlockSpec((8, tn), lambda j: (0, j)),
        out_shape=jax.ShapeDtypeStruct((8, n), F32),
        compiler_params=_params(1, 48),
        name="ada",
    )(c8, w, b.reshape(1, n))
    return out[:bsz]


def _norm_mod_kernel(*refs, has_post, has_pre, res_w):
    refs = list(refs)
    x = refs.pop(0)[...]
    if has_post:
        y_ref, gpost_ref, gate_ref = refs.pop(0), refs.pop(0), refs.pop(0)
    if has_pre:
        gpre_ref, scale_ref, shift_ref = refs.pop(0), refs.pop(0), refs.pop(0)
    if has_post:
        x = x + res_w * gate_ref[...] * _rms(y_ref[...].astype(F32), gpost_ref[...])
        refs.pop(0)[...] = x
    if has_pre:
        refs.pop(0)[...] = (_rms(x, gpre_ref[...]) * (1.0 + scale_ref[...]) + shift_ref[...]).astype(BF16)


def _norm_mod(x, post=None, pre=None, res_w=1.0):
    bsz, s, d = x.shape
    ts = 256
    row = pl.BlockSpec((None, ts, d), lambda b, t: (b, t, 0))
    gain = pl.BlockSpec((1, d), lambda b, t: (0, 0))
    mod = pl.BlockSpec((None, 1, d), lambda b, t: (b, 0, 0))
    args, in_specs, out_specs, out_shape = [x], [row], [], []
    if post is not None:
        y, g_post, gate = post
        args += [y, g_post.reshape(1, d), gate]
        in_specs += [row, gain, mod]
        out_specs.append(row)
        out_shape.append(jax.ShapeDtypeStruct((bsz, s, d), F32))
    if pre is not None:
        g_pre, scale, shift = pre
        args += [g_pre.reshape(1, d), scale, shift]
        in_specs += [gain, mod, mod]
        out_specs.append(row)
        out_shape.append(jax.ShapeDtypeStruct((bsz, s, d), BF16))
    return pl.pallas_call(
        functools.partial(_norm_mod_kernel, has_post=post is not None, has_pre=pre is not None, res_w=res_w),
        grid=(bsz, s // ts),
        in_specs=in_specs, out_specs=out_specs, out_shape=out_shape,
        compiler_params=_params(2, 48),
        name="norm_mod",
    )(*args)


def _gate_up_kernel(h_ref, wg_ref, wu_ref, o_ref):
    h = h_ref[...]
    g = jnp.dot(h, wg_ref[...], preferred_element_type=F32)
    u = jnp.dot(h, wu_ref[...], preferred_element_type=F32)
    o_ref[...] = (g * jax.nn.sigmoid(g) * u).astype(o_ref.dtype)


def _gate_up(h, wg, wu):
    m, k = h.shape
    f = wg.shape[1]
    tm, tf = 1024, 512
    return pl.pallas_call(
        _gate_up_kernel,
        grid=(m // tm, f // tf),
        in_specs=[pl.BlockSpec((tm, k), lambda i, j: (i, 0)),
                  pl.BlockSpec((k, tf), lambda i, j: (0, j)),
                  pl.BlockSpec((k, tf), lambda i, j: (0, j))],
        out_specs=pl.BlockSpec((tm, tf), lambda i, j: (i, j)),
        out_shape=jax.ShapeDtypeStruct((m, f), BF16),
        compiler_params=_params(2, 56),
        name="gate_up",
    )(h, wg, wu)


def _mm_kernel(*refs, splits):
    w_ref, o_ref = refs[len(splits)], refs[-1]
    acc, off = None, 0
    for a_ref, k in zip(refs, splits):
        part = jnp.dot(a_ref[...], w_ref[off:off + k, :], preferred_element_type=F32)
        acc = part if acc is None else acc + part
        off += k
    o_ref[...] = acc.astype(o_ref.dtype)


def _matmul(a_parts, w, tm, tn, vmem_mib, name):
    m = a_parts[0].shape[0]
    n = w.shape[1]
    splits = tuple(a.shape[1] for a in a_parts)
    return pl.pallas_call(
        functools.partial(_mm_kernel, splits=splits),
        grid=(m // tm, n // tn),
        in_specs=[pl.BlockSpec((tm, k), lambda i, j: (i, 0)) for k in splits]
        + [pl.BlockSpec((sum(splits), tn), lambda i, j: (0, j))],
        out_specs=pl.BlockSpec((tm, tn), lambda i, j: (i, j)),
        out_shape=jax.ShapeDtypeStruct((m, n), BF16),
        compiler_params=_params(2, vmem_mib),
        name=name,
    )(*a_parts, w)


SWA_TQ = 512


def _swa_kernel(sink_ref, slope_ref, q_ref, kown_ref, vown_ref, kprev_ref, vprev_ref, g_ref, o_ref, o_scr):
    t = pl.program_id(1)
    group = N_HEADS_A // N_KV_A
    scale = HEAD_DIM ** -0.5
    kj = lax.broadcasted_iota(jnp.int32, (WINDOW, 2 * WINDOW), 1)
    dist = lax.broadcasted_iota(jnp.int32, (WINDOW, 2 * WINDOW), 0) + WINDOW - kj
    in_window = (dist >= 0) & (dist < WINDOW)
    distf = dist.astype(F32)
    first_key = jnp.where(t > 0, 0, WINDOW)
    for s in range(SWA_TQ // WINDOW):
        rows = slice(s * WINDOW, (s + 1) * WINDOW)
        if s == 0:
            kk = jnp.concatenate([kprev_ref[...], kown_ref[0:WINDOW, :]], axis=0)
            vv = jnp.concatenate([vprev_ref[...], vown_ref[0:WINDOW, :]], axis=0)
            valid = in_window & (kj >= first_key)
        else:
            kk = kown_ref[(s - 1) * WINDOW:(s + 1) * WINDOW, :]
            vv = vown_ref[(s - 1) * WINDOW:(s + 1) * WINDOW, :]
            valid = in_window
        for h in range(N_HEADS_A):
            kv = h // group
            cols = slice(h * HEAD_DIM, (h + 1) * HEAD_DIM)
            kvc = slice(kv * HEAD_DIM, (kv + 1) * HEAD_DIM)
            sc = lax.dot_general(q_ref[rows, cols], kk[:, kvc], NT_DIMS, preferred_element_type=F32) * scale
            sc = jnp.where(valid, sc - slope_ref[h] * distf, NEG_INF)
            sink = sink_ref[h]
            mx = jnp.maximum(jnp.max(sc, axis=-1, keepdims=True), sink)
            p = jnp.exp(sc - mx)
            denom = jnp.sum(p, axis=-1, keepdims=True) + jnp.exp(sink - mx)
            o_scr[rows, cols] = jnp.dot((p / denom).astype(BF16), vv[:, kvc], preferred_element_type=F32)
    o_ref[...] = _rms(o_scr[...], g_ref[...]).astype(o_ref.dtype)


def _swa(proj, sinks, slopes, norm_a):
    bsz, s, _ = proj.shape
    wpt = SWA_TQ // WINDOW
    smem = pl.BlockSpec(memory_space=pltpu.SMEM)
    own = lambda col: (lambda b, t: (b, t, col))
    prev = lambda col: (lambda b, t: (b, jnp.maximum(t * wpt - 1, 0), col))
    return pl.pallas_call(
        _swa_kernel,
        grid=(bsz, s // SWA_TQ),
        in_specs=[smem, smem,
                  pl.BlockSpec((None, SWA_TQ, WIDTH_A), own(OFF_QA // WIDTH_A)),
                  pl.BlockSpec((None, SWA_TQ, KV_A), own(OFF_KA // KV_A)),
                  pl.BlockSpec((None, SWA_TQ, KV_A), own(OFF_VA // KV_A)),
                  pl.BlockSpec((None, WINDOW, KV_A), prev(OFF_KA // KV_A)),
                  pl.BlockSpec((None, WINDOW, KV_A), prev(OFF_VA // KV_A)),
                  pl.BlockSpec((1, WIDTH_A), lambda b, t: (0, 0))],
        out_specs=pl.BlockSpec((None, SWA_TQ, WIDTH_A), lambda b, t: (b, t, 0)),
        out_shape=jax.ShapeDtypeStruct((bsz, s, WIDTH_A), BF16),
        scratch_shapes=[pltpu.VMEM((SWA_TQ, WIDTH_A), F32)],
        compiler_params=_params(2, 40),
        name="swa",
    )(sinks, slopes, proj, proj, proj, proj, proj, norm_a.reshape(1, WIDTH_A))


def _moba_kernel(slope_ref, q_ref, k_ref, v_ref, g_ref, o_ref, o_scr):
    n = pl.program_id(1)
    j = pl.program_id(2)
    group = N_HEADS_B // N_KV_B
    nblk = k_ref.shape[0] // MOBA_BLOCK
    rows = group * MOBA_BLOCK
    scale = HEAD_DIM ** -0.5

    qs = jnp.concatenate([q_ref[:, g * HEAD_DIM:(g + 1) * HEAD_DIM] for g in range(group)], axis=0)

    kmean = jnp.concatenate(
        [jnp.sum(k_ref[i * MOBA_BLOCK:(i + 1) * MOBA_BLOCK, :].astype(F32), axis=0, keepdims=True)
         for i in range(nblk)], axis=0) * (1.0 / MOBA_BLOCK)
    km_hi = kmean.astype(BF16)
    km_lo = (kmean - km_hi.astype(F32)).astype(BF16)
    g2 = lax.dot_general(jnp.concatenate([km_hi, km_lo], axis=0), qs, NT_DIMS, preferred_element_type=F32)
    gate = g2[:nblk] + g2[nblk:]

    blk = lax.broadcasted_iota(jnp.int32, (nblk, rows), 0)
    beaten = jnp.zeros((nblk, rows), F32)
    for i in range(nblk):
        gi = gate[i:i + 1, :]
        beats = (gi > gate) | ((gi == gate) & (blk > i))
        beaten = beaten + jnp.where(beats, jnp.where(i < n, 1.0, 0.0), 0.0)
    chosen = (blk < n) & (beaten < MOBA_TOPK)
    bias_t = jnp.where(chosen, 0.0, NEG_INF)
    bias = jnp.concatenate([bias_t, jnp.zeros((HEAD_DIM - nblk, rows), F32)], axis=0).T
    q_ext = jnp.concatenate([qs, bias.astype(BF16)], axis=1)

    tq = lax.broadcasted_iota(jnp.int32, (rows, MOBA_BLOCK), 0) & (MOBA_BLOCK - 1)
    d_own = (tq - lax.broadcasted_iota(jnp.int32, (rows, MOBA_BLOCK), 1)).astype(F32)
    slope = jnp.concatenate(
        [jnp.full((MOBA_BLOCK, MOBA_BLOCK), slope_ref[j * group + g], F32) for g in range(group)], axis=0)
    alibi_own = slope * d_own

    start = pl.multiple_of(n * MOBA_BLOCK, MOBA_BLOCK)
    sc = lax.dot_general(qs, k_ref[pl.ds(start, MOBA_BLOCK), :], NT_DIMS, preferred_element_type=F32) * scale
    sc = jnp.where(d_own >= 0, sc - alibi_own, NEG_INF)
    m0 = jnp.max(sc, axis=-1, keepdims=True)
    p = jnp.exp(sc - m0)
    l0 = jnp.sum(p, axis=-1, keepdims=True)
    acc0 = jnp.dot(p.astype(BF16), v_ref[pl.ds(start, MOBA_BLOCK), :], preferred_element_type=F32)

    lane = lax.broadcasted_iota(jnp.int32, (MOBA_BLOCK, HEAD_DIM), 1)

    def past_block(i, carry):
        m, l, acc = carry
        st = pl.multiple_of(i * MOBA_BLOCK, MOBA_BLOCK)
        k_ext = jnp.concatenate([k_ref[pl.ds(st, MOBA_BLOCK), :], (lane == i).astype(BF16)], axis=1)
        far = ((n - i) * MOBA_BLOCK).astype(F32)
        s_i = lax.dot_general(q_ext, k_ext, NT_DIMS, preferred_element_type=F32) * scale
        s_i = s_i - (alibi_own + slope * far)
        m_new = jnp.maximum(m, jnp.max(s_i, axis=-1, keepdims=True))
        alpha = jnp.exp(m - m_new)
        p_i = jnp.exp(s_i - m_new)
        l = alpha * l + jnp.sum(p_i, axis=-1, keepdims=True)
        acc = alpha * acc + jnp.dot(p_i.astype(BF16), v_ref[pl.ds(st, MOBA_BLOCK), :],
                                    preferred_element_type=F32)
        return m_new, l, acc

    _, l, acc = lax.fori_loop(0, n, past_block, (m0, l0, acc0))
    o = acc / l
    for g in range(group):
        o_scr[j, :, g * HEAD_DIM:(g + 1) * HEAD_DIM] = o[g * MOBA_BLOCK:(g + 1) * MOBA_BLOCK, :]

    @pl.when(j == N_KV_B - 1)
    def _():
        y = jnp.concatenate([o_scr[jj] for jj in range(N_KV_B)], axis=1)
        o_ref[...] = _rms(y, g_ref[...]).astype(o_ref.dtype)


def _moba(proj, slopes, norm_b):
    bsz, s, _ = proj.shape
    hw = (N_HEADS_B // N_KV_B) * HEAD_DIM
    return pl.pallas_call(
        _moba_kernel,
        grid=(bsz, s // MOBA_BLOCK, N_KV_B),
        in_specs=[pl.BlockSpec(memory_space=pltpu.SMEM),
                  pl.BlockSpec((None, MOBA_BLOCK, hw), lambda b, n, j: (b, n, OFF_QB // hw + j)),
                  pl.BlockSpec((None, s, HEAD_DIM), lambda b, n, j: (b, 0, OFF_KB // HEAD_DIM + j)),
                  pl.BlockSpec((None, s, HEAD_DIM), lambda b, n, j: (b, 0, OFF_VB // HEAD_DIM + j)),
                  pl.BlockSpec((1, WIDTH_B), lambda b, n, j: (0, 0))],
        out_specs=pl.BlockSpec((None, MOBA_BLOCK, WIDTH_B), lambda b, n, j: (b, n, 0)),
        out_shape=jax.ShapeDtypeStruct((bsz, s, WIDTH_B), BF16),
        scratch_shapes=[pltpu.VMEM((N_KV_B, MOBA_BLOCK, hw), F32)],
        compiler_params=_params(3, 40),
        name="moba",
    )(slopes, proj, proj, proj, norm_b.reshape(1, WIDTH_B))


def _alibi_slopes(n):
    return 2.0 ** (-8.0 * jnp.arange(1, n + 1, dtype=F32) / n)


def _pad_ff(w, axis):
    pad = [(0, 0), (0, 0)]
    pad[axis] = (0, D_FF_PAD - D_FF)
    return jnp.pad(w.astype(BF16), pad)


def _ffn(h, w_gate, w_up, w_down):
    act = _gate_up(h, _pad_ff(w_gate, 1), _pad_ff(w_up, 1))
    return _matmul([act], _pad_ff(w_down, 0), tm=512, tn=512, vmem_mib=58, name="ffn_down")


def kernel(x, c, w_ada, b_ada, ffn1_norm_pre, ffn1_norm_post, ffn1_w_gate, ffn1_w_up, ffn1_w_down, mix_norm_pre, mix_norm_post, w_in, attn_sinks, norm_a, norm_b, w_out, ffn2_norm_pre, ffn2_norm_post, ffn2_w_gate, ffn2_w_up, ffn2_w_down):
    bsz, s, d = x.shape
    depth = w_ada.shape[0]
    slopes_a = _alibi_slopes(N_HEADS_A)
    slopes_b = _alibi_slopes(N_HEADS_B)
    for l in range(depth):
        mod = _ada(c, w_ada[l], b_ada[l]).reshape(bsz, N_MOD, 1, d)
        shift, scale, gate = (lambda i: mod[:, 3 * i]), (lambda i: mod[:, 3 * i + 1]), (lambda i: mod[:, 3 * i + 2])

        h = _norm_mod(x, pre=(ffn1_norm_pre[l], scale(0), shift(0)))[0]
        y = _ffn(h.reshape(bsz * s, d), ffn1_w_gate[l], ffn1_w_up[l], ffn1_w_down[l])

        x, h = _norm_mod(x, post=(y.reshape(bsz, s, d), ffn1_norm_post[l], gate(0)),
                         pre=(mix_norm_pre[l], scale(1), shift(1)), res_w=0.5)
        wl = w_in[l]
        w_in_p = jnp.concatenate(
            [wl[:, 0:WIDTH_A],
             wl[:, WIDTH_A + 2 * KV_A:WIDTH_A + 2 * KV_A + WIDTH_B],
             wl[:, WIDTH_A:WIDTH_A + 2 * KV_A],
             wl[:, WIDTH_A + 2 * KV_A + WIDTH_B:]], axis=1).astype(BF16)
        proj = _matmul([h.reshape(bsz * s, d)], w_in_p, tm=1024, tn=512, vmem_mib=40, name="w_in")
        proj = proj.reshape(bsz, s, D_IN)
        ya = _swa(proj, attn_sinks[l], slopes_a, norm_a[l])
        yb = _moba(proj, slopes_b, norm_b[l])
        y = _matmul([ya.reshape(bsz * s, WIDTH_A), yb.reshape(bsz * s, WIDTH_B)], w_out[l].astype(BF16),
                    tm=1024, tn=512, vmem_mib=40, name="w_out")

        x, h = _norm_mod(x, post=(y.reshape(bsz, s, d), mix_norm_post[l], gate(1)),
                         pre=(ffn2_norm_pre[l], scale(2), shift(2)), res_w=1.0)
        y = _ffn(h.reshape(bsz * s, d), ffn2_w_gate[l], ffn2_w_up[l], ffn2_w_down[l])
        x = _norm_mod(x, post=(y.reshape(bsz, s, d), ffn2_norm_post[l], gate(2)), res_w=0.5)[0]
    return x
```

```python
import functools
import math

import jax
import jax.numpy as jnp
from jax import lax
from jax.experimental import pallas as pl
from jax.experimental.pallas import tpu as pltpu

D_MODEL = 4096
HEAD_DIM = 128
N_HEADS_A = 16
N_KV_A = 2
WINDOW = 128
N_HEADS_B = 16
N_KV_B = 4
MOBA_BLOCK = 256
MOBA_TOPK = 3
D_FF = 11008
EPS = 1e-6
NEG_INF = -1e30
N_MOD = 9

WIDTH_A = N_HEADS_A * HEAD_DIM
WIDTH_B = N_HEADS_B * HEAD_DIM
KV_A = N_KV_A * HEAD_DIM
KV_B = N_KV_B * HEAD_DIM
D_IN = WIDTH_A + 2 * KV_A + WIDTH_B + 2 * KV_B

OFF_QA = 0
OFF_QB = OFF_QA + WIDTH_A
OFF_KA = OFF_QB + WIDTH_B
OFF_VA = OFF_KA + KV_A
OFF_KB = OFF_VA + KV_A
OFF_VB = OFF_KB + KV_B

MIB = 1024 * 1024
F32 = jnp.float32
BF16 = jnp.bfloat16
NT_DIMS = (((1,), (1,)), ((), ()))


def _params(n_grid, vmem_mib):
    return pltpu.CompilerParams(dimension_semantics=("arbitrary",) * n_grid,
                                vmem_limit_bytes=vmem_mib * MIB)


def _rms(v, g):
    return v * lax.rsqrt(jnp.mean(v * v, axis=-1, keepdims=True) + EPS) * g


def _ada_kernel(c_ref, w_ref, b_ref, o_ref):
    c = c_ref[...]
    s = c * jax.nn.sigmoid(c)
    hi = s.astype(BF16)
    lo = (s - hi.astype(F32)).astype(BF16)
    lhs = jnp.concatenate([hi, lo], axis=0)
    r = jnp.dot(lhs, w_ref[...].astype(BF16), preferred_element_type=F32)
    o_ref[...] = r[:8] + r[8:] + b_ref[...]


def _ada(c, w, b):
    bsz, d = c.shape
    n = w.shape[1]
    tn = 1024
    c8 = jnp.pad(c, ((0, 8 - bsz), (0, 0)))
    out = pl.pallas_call(
        _ada_kernel,
        grid=(n // tn,),
        in_specs=[pl.BlockSpec((8, d), lambda j: (0, 0)),
                  pl.BlockSpec((d, tn), lambda j: (0, j)),
                  pl.BlockSpec((1, tn), lambda j: (0, j))],
        out_specs=pl.BlockSpec((8, tn), lambda j: (0, j)),
        out_shape=jax.ShapeDtypeStruct((8, n), F32),
        compiler_params=_params(1, 48),
        name="ada",
    )(c8, w, b.reshape(1, n))
    return out[:bsz]


def _norm_mod_kernel(*refs, has_post, has_pre, res_w):
    refs = list(refs)
    x = refs.pop(0)[...]
    if has_post:
        y_ref, gpost_ref, gate_ref = refs.pop(0), refs.pop(0), refs.pop(0)
    if has_pre:
        gpre_ref, scale_ref, shift_ref = refs.pop(0), refs.pop(0), refs.pop(0)
    if has_post:
        x = x + res_w * gate_ref[...] * _rms(y_ref[...].astype(F32), gpost_ref[...])
        refs.pop(0)[...] = x
    if has_pre:
        refs.pop(0)[...] = (_rms(x, gpre_ref[...]) * (1.0 + scale_ref[...]) + shift_ref[...]).astype(BF16)


def _norm_mod(x, post=None, pre=None, res_w=1.0):
    bsz, s, d = x.shape
    ts = 256
    row = pl.BlockSpec((None, ts, d), lambda b, t: (b, t, 0))
    gain = pl.BlockSpec((1, d), lambda b, t: (0, 0))
    mod = pl.BlockSpec((None, 1, d), lambda b, t: (b, 0, 0))
    args, in_specs, out_specs, out_shape = [x], [row], [], []
    if post is not None:
        y, g_post, gate = post
        args += [y, g_post.reshape(1, d), gate]
        in_specs += [row, gain, mod]
        out_specs.append(row)
        out_shape.append(jax.ShapeDtypeStruct((bsz, s, d), F32))
    if pre is not None:
        g_pre, scale, shift = pre
        args += [g_pre.reshape(1, d), scale, shift]
        in_specs += [gain, mod, mod]
        out_specs.append(row)
        out_shape.append(jax.ShapeDtypeStruct((bsz, s, d), BF16))
    return pl.pallas_call(
        functools.partial(_norm_mod_kernel, has_post=post is not None, has_pre=pre is not None, res_w=res_w),
        grid=(bsz, s // ts),
        in_specs=in_specs, out_specs=out_specs, out_shape=out_shape,
        compiler_params=_params(2, 48),
        name="norm_mod",
    )(*args)


def _gate_up_kernel(h_ref, wg_ref, wu_ref, o_ref):
    h = h_ref[...]
    g = jnp.dot(h, wg_ref[...], preferred_element_type=F32)
    u = jnp.dot(h, wu_ref[...], preferred_element_type=F32)
    o_ref[...] = (g * jax.nn.sigmoid(g) * u).astype(o_ref.dtype)


def _gate_up(h, wg, wu):
    m, k = h.shape
    f = wg.shape[1]
    tm, tf = 2048, 256
    return pl.pallas_call(
        _gate_up_kernel,
        grid=(m // tm, f // tf),
        in_specs=[pl.BlockSpec((tm, k), lambda i, j: (i, 0)),
                  pl.BlockSpec((k, tf), lambda i, j: (0, j)),
                  pl.BlockSpec((k, tf), lambda i, j: (0, j))],
        out_specs=pl.BlockSpec((tm, tf), lambda i, j: (i, j)),
        out_shape=jax.ShapeDtypeStruct((m, f), BF16),
        compiler_params=_params(2, 56),
        name="gate_up",
    )(h, wg, wu)


def _mm_kernel(*refs, splits):
    w_ref, o_ref = refs[len(splits)], refs[-1]
    acc, off = None, 0
    for a_ref, k in zip(refs, splits):
        part = jnp.dot(a_ref[...], w_ref[off:off + k, :], preferred_element_type=F32)
        acc = part if acc is None else acc + part
        off += k
    o_ref[...] = acc.astype(o_ref.dtype)


def _matmul(a_parts, w, tm, tn, vmem_mib, name):
    m = a_parts[0].shape[0]
    n = w.shape[1]
    splits = tuple(a.shape[1] for a in a_parts)
    return pl.pallas_call(
        functools.partial(_mm_kernel, splits=splits),
        grid=(m // tm, n // tn),
        in_specs=[pl.BlockSpec((tm, k), lambda i, j: (i, 0)) for k in splits]
        + [pl.BlockSpec((sum(splits), tn), lambda i, j: (0, j))],
        out_specs=pl.BlockSpec((tm, tn), lambda i, j: (i, j)),
        out_shape=jax.ShapeDtypeStruct((m, n), BF16),
        compiler_params=_params(2, vmem_mib),
        name=name,
    )(*a_parts, w)


SWA_TQ = 512


def _swa_kernel(sink_ref, slope_ref, q_ref, kown_ref, vown_ref, kprev_ref, vprev_ref, g_ref, o_ref, o_scr):
    t = pl.program_id(1)
    group = N_HEADS_A // N_KV_A
    scale = HEAD_DIM ** -0.5
    kj = lax.broadcasted_iota(jnp.int32, (WINDOW, 2 * WINDOW), 1)
    dist = lax.broadcasted_iota(jnp.int32, (WINDOW, 2 * WINDOW), 0) + WINDOW - kj
    in_window = (dist >= 0) & (dist < WINDOW)
    distf = dist.astype(F32)
    first_key = jnp.where(t > 0, 0, WINDOW)
    for s in range(SWA_TQ // WINDOW):
        rows = slice(s * WINDOW, (s + 1) * WINDOW)
        if s == 0:
            kk = jnp.concatenate([kprev_ref[...], kown_ref[0:WINDOW, :]], axis=0)
            vv = jnp.concatenate([vprev_ref[...], vown_ref[0:WINDOW, :]], axis=0)
            valid = in_window & (kj >= first_key)
        else:
            kk = kown_ref[(s - 1) * WINDOW:(s + 1) * WINDOW, :]
            vv = vown_ref[(s - 1) * WINDOW:(s + 1) * WINDOW, :]
            valid = in_window
        for h in range(N_HEADS_A):
            kv = h // group
            cols = slice(h * HEAD_DIM, (h + 1) * HEAD_DIM)
            kvc = slice(kv * HEAD_DIM, (kv + 1) * HEAD_DIM)
            sc = lax.dot_general(q_ref[rows, cols], kk[:, kvc], NT_DIMS, preferred_element_type=F32) * scale
            sc = jnp.where(valid, sc - slope_ref[h] * distf, NEG_INF)
            sink = sink_ref[h]
            mx = jnp.maximum(jnp.max(sc, axis=-1, keepdims=True), sink)
            p = jnp.exp(sc - mx)
            denom = jnp.sum(p, axis=-1, keepdims=True) + jnp.exp(sink - mx)
            o_scr[rows, cols] = jnp.dot((p / denom).astype(BF16), vv[:, kvc], preferred_element_type=F32)
    o_ref[...] = _rms(o_scr[...], g_ref[...]).astype(o_ref.dtype)


def _swa(proj, sinks, slopes, norm_a):
    bsz, s, _ = proj.shape
    wpt = SWA_TQ // WINDOW
    smem = pl.BlockSpec(memory_space=pltpu.SMEM)
    own = lambda col: (lambda b, t: (b, t, col))
    prev = lambda col: (lambda b, t: (b, jnp.maximum(t * wpt - 1, 0), col))
    return pl.pallas_call(
        _swa_kernel,
        grid=(bsz, s // SWA_TQ),
        in_specs=[smem, smem,
                  pl.BlockSpec((None, SWA_TQ, WIDTH_A), own(OFF_QA // WIDTH_A)),
                  pl.BlockSpec((None, SWA_TQ, KV_A), own(OFF_KA // KV_A)),
                  pl.BlockSpec((None, SWA_TQ, KV_A), own(OFF_VA // KV_A)),
                  pl.BlockSpec((None, WINDOW, KV_A), prev(OFF_KA // KV_A)),
                  pl.BlockSpec((None, WINDOW, KV_A), prev(OFF_VA // KV_A)),
                  pl.BlockSpec((1, WIDTH_A), lambda b, t: (0, 0))],
        out_specs=pl.BlockSpec((None, SWA_TQ, WIDTH_A), lambda b, t: (b, t, 0)),
        out_shape=jax.ShapeDtypeStruct((bsz, s, WIDTH_A), BF16),
        scratch_shapes=[pltpu.VMEM((SWA_TQ, WIDTH_A), F32)],
        compiler_params=_params(2, 40),
        name="swa",
    )(sinks, slopes, proj, proj, proj, proj, proj, norm_a.reshape(1, WIDTH_A))


MOBA_GROUP = N_HEADS_B // N_KV_B
N_ALIBI_TERMS = 3


def _moba_kernel(slope_ref, q_ref, k_ref, vt_ref, g_ref, o_ref,
                 kmean_scr, qext_scr, bias_scr, m_scr, l_scr, acc_scr, y_scr):
    n = pl.program_id(1)
    blk_sz = MOBA_BLOCK
    nblk = k_ref.shape[0] // blk_sz
    qrows = MOBA_GROUP * blk_sz
    scale = HEAD_DIM ** -0.5
    to_log2 = scale * math.log2(math.e)

    @pl.when(n == 0)
    def _():
        kmean = jnp.concatenate(
            [jnp.sum(k_ref[i * blk_sz:(i + 1) * blk_sz, :].astype(F32), axis=0, keepdims=True)
             for i in range(nblk)], axis=0) * (1.0 / blk_sz)
        hi = kmean.astype(BF16)
        lo = (kmean - hi.astype(F32)).astype(BF16)
        for j in range(N_KV_B):
            kc = slice(j * HEAD_DIM, (j + 1) * HEAD_DIM)
            kmean_scr[j] = jnp.concatenate([hi[:, kc], lo[:, kc]], axis=0)

    lane = lax.broadcasted_iota(jnp.int32, (blk_sz, HEAD_DIM), 1)
    key_off = lax.broadcasted_iota(jnp.int32, (blk_sz, HEAD_DIM), 0).astype(F32)
    off_lanes = lane < N_ALIBI_TERMS
    far_lanes = (lane >= N_ALIBI_TERMS) & (lane < 2 * N_ALIBI_TERMS)
    own_feat = jnp.where(off_lanes, key_off, 0.0).astype(BF16)
    causal = jnp.where(lax.broadcasted_iota(jnp.int32, (blk_sz, blk_sz), 0)
                       <= lax.broadcasted_iota(jnp.int32, (blk_sz, blk_sz), 1), 0.0, NEG_INF)
    blk = lax.broadcasted_iota(jnp.int32, (nblk, qrows), 0)

    head_cols = [slice(h * HEAD_DIM, (h + 1) * HEAD_DIM) for h in range(N_HEADS_B)]
    kv_cols = [slice(j * HEAD_DIM, (j + 1) * HEAD_DIM) for j in range(N_KV_B)]

    for j in range(N_KV_B):
        heads = head_cols[j * MOBA_GROUP:(j + 1) * MOBA_GROUP]
        qs = jnp.concatenate([q_ref[:, hc] for hc in heads], axis=0)
        g2 = lax.dot_general(kmean_scr[j], qs, NT_DIMS, preferred_element_type=F32)
        gate = g2[:nblk] + g2[nblk:]
        beaten = jnp.zeros((nblk, qrows), F32)
        for i in range(nblk):
            gi = gate[i:i + 1, :]
            beats = (gi > gate) | ((gi == gate) & (blk > i))
            beaten = beaten + jnp.where(beats, jnp.where(i < n, 1.0, 0.0), 0.0)
        bias_t = jnp.where((blk < n) & (beaten < MOBA_TOPK), 0.0, NEG_INF)

        for g in range(MOBA_GROUP):
            h = j * MOBA_GROUP + g
            bias_scr[h] = bias_t[:, g * blk_sz:(g + 1) * blk_sz]
            a = jnp.full((blk_sz, HEAD_DIM), slope_ref[h], F32) * (1.0 / scale)
            a1 = a.astype(BF16).astype(F32)
            a2 = (a - a1).astype(BF16).astype(F32)
            a3 = (a - a1 - a2).astype(BF16).astype(F32)
            term = jnp.where((lane == 0) | (lane == 3), a1, jnp.where((lane == 1) | (lane == 4), a2, a3))
            feat = jnp.where(lane < 2 * N_ALIBI_TERMS, term, 0.0).astype(BF16)
            qext_scr[h] = jnp.concatenate([q_ref[:, heads[g]], feat], axis=1)

    def attend(i, own):
        st = pl.multiple_of(i * blk_sz, blk_sz)
        if own:
            kfeat = own_feat
        else:
            far = ((n - i) * blk_sz).astype(F32)
            kfeat = jnp.where(far_lanes, -far, jnp.where(off_lanes, key_off, 0.0)).astype(BF16)
        logits = []
        for j in range(N_KV_B):
            k_ext = jnp.concatenate([k_ref[pl.ds(st, blk_sz), kv_cols[j]], kfeat], axis=1)
            for g in range(MOBA_GROUP):
                logits.append(lax.dot_general(k_ext, qext_scr[j * MOBA_GROUP + g], NT_DIMS,
                                              preferred_element_type=F32))
        for h in range(N_HEADS_B):
            t = logits[h] * to_log2
            if own:
                t = t + causal
                m_new = jnp.max(t, axis=0, keepdims=True)
            else:
                t = t + bias_scr[h, pl.ds(i, 1), :]
                m_old = m_scr[h][0:1, :]
                m_new = jnp.maximum(m_old, jnp.max(t, axis=0, keepdims=True))
            p = jnp.exp2(t - m_new)
            l_new = jnp.sum(p, axis=0, keepdims=True)
            acc = jnp.dot(vt_ref[i, kv_cols[h // MOBA_GROUP], :], p.astype(BF16), preferred_element_type=F32)
            if not own:
                alpha = jnp.exp2(m_old - m_new)
                l_new = alpha * l_scr[h][0:1, :] + l_new
                acc = alpha * acc_scr[h] + acc
            m_scr[h] = jnp.broadcast_to(m_new, (8, blk_sz))
            l_scr[h] = jnp.broadcast_to(l_new, (8, blk_sz))
            acc_scr[h] = acc

    attend(n, own=True)

    def past_block(i, carry):
        attend(i, own=False)
        return carry

    lax.fori_loop(0, n, past_block, 0)

    for h in range(N_HEADS_B):
        y_scr[:, head_cols[h]] = (acc_scr[h] / l_scr[h][0:1, :]).T
    o_ref[...] = _rms(y_scr[...], g_ref[...]).astype(o_ref.dtype)


def _moba(proj, slopes, norm_b):
    bsz, s, _ = proj.shape
    nblk = s // MOBA_BLOCK
    v_t = proj[:, :, OFF_VB:OFF_VB + KV_B].reshape(bsz, nblk, MOBA_BLOCK, KV_B).transpose(0, 1, 3, 2)
    return pl.pallas_call(
        _moba_kernel,
        grid=(bsz, nblk),
        in_specs=[pl.BlockSpec(memory_space=pltpu.SMEM),
                  pl.BlockSpec((None, MOBA_BLOCK, WIDTH_B), lambda b, n: (b, n, OFF_QB // WIDTH_B)),
                  pl.BlockSpec((None, s, KV_B), lambda b, n: (b, 0, OFF_KB // KV_B)),
                  pl.BlockSpec((None, nblk, KV_B, MOBA_BLOCK), lambda b, n: (b, 0, 0, 0)),
                  pl.BlockSpec((1, WIDTH_B), lambda b, n: (0, 0))],
        out_specs=pl.BlockSpec((None, MOBA_BLOCK, WIDTH_B), lambda b, n: (b, n, 0)),
        out_shape=jax.ShapeDtypeStruct((bsz, s, WIDTH_B), BF16),
        scratch_shapes=[pltpu.VMEM((N_KV_B, 2 * nblk, HEAD_DIM), BF16),
                        pltpu.VMEM((N_HEADS_B, MOBA_BLOCK, 2 * HEAD_DIM), BF16),
                        pltpu.VMEM((N_HEADS_B, nblk, MOBA_BLOCK), F32),
                        pltpu.VMEM((N_HEADS_B, 8, MOBA_BLOCK), F32),
                        pltpu.VMEM((N_HEADS_B, 8, MOBA_BLOCK), F32),
                        pltpu.VMEM((N_HEADS_B, HEAD_DIM, MOBA_BLOCK), F32),
                        pltpu.VMEM((MOBA_BLOCK, WIDTH_B), F32)],
        compiler_params=_params(2, 40),
        name="moba",
    )(slopes, proj, proj, v_t, norm_b.reshape(1, WIDTH_B))


def _alibi_slopes(n):
    return 2.0 ** (-8.0 * jnp.arange(1, n + 1, dtype=F32) / n)


def _ffn(h, w_gate, w_up, w_down):
    act = _gate_up(h, w_gate.astype(BF16), w_up.astype(BF16))
    return _matmul([act], w_down.astype(BF16), tm=512, tn=512, vmem_mib=58, name="ffn_down")


def kernel(x, c, w_ada, b_ada, ffn1_norm_pre, ffn1_norm_post, ffn1_w_gate, ffn1_w_up, ffn1_w_down, mix_norm_pre, mix_norm_post, w_in, attn_sinks, norm_a, norm_b, w_out, ffn2_norm_pre, ffn2_norm_post, ffn2_w_gate, ffn2_w_up, ffn2_w_down):
    bsz, s, d = x.shape
    depth = w_ada.shape[0]
    slopes_a = _alibi_slopes(N_HEADS_A)
    slopes_b = _alibi_slopes(N_HEADS_B)
    for l in range(depth):
        mod = _ada(c, w_ada[l], b_ada[l]).reshape(bsz, N_MOD, 1, d)
        shift, scale, gate = (lambda i: mod[:, 3 * i]), (lambda i: mod[:, 3 * i + 1]), (lambda i: mod[:, 3 * i + 2])

        h = _norm_mod(x, pre=(ffn1_norm_pre[l], scale(0), shift(0)))[0]
        y = _ffn(h.reshape(bsz * s, d), ffn1_w_gate[l], ffn1_w_up[l], ffn1_w_down[l])

        x, h = _norm_mod(x, post=(y.reshape(bsz, s, d), ffn1_norm_post[l], gate(0)),
                         pre=(mix_norm_pre[l], scale(1), shift(1)), res_w=0.5)
        wl = w_in[l]
        w_in_p = jnp.concatenate(
            [wl[:, 0:WIDTH_A],
             wl[:, WIDTH_A + 2 * KV_A:WIDTH_A + 2 * KV_A + WIDTH_B],
             wl[:, WIDTH_A:WIDTH_A + 2 * KV_A],
             wl[:, WIDTH_A + 2 * KV_A + WIDTH_B:]], axis=1).astype(BF16)
        proj = _matmul([h.reshape(bsz * s, d)], w_in_p, tm=1024, tn=512, vmem_mib=40, name="w_in")
        proj = proj.reshape(bsz, s, D_IN)
        ya = _swa(proj, attn_sinks[l], slopes_a, norm_a[l])
        yb = _moba(proj, slopes_b, norm_b[l])
        y = _matmul([ya.reshape(bsz * s, WIDTH_A), yb.reshape(bsz * s, WIDTH_B)], w_out[l].astype(BF16),
                    tm=1024, tn=512, vmem_mib=40, name="w_out")

        x, h = _norm_mod(x, post=(y.reshape(bsz, s, d), mix_norm_post[l], gate(1)),
                         pre=(ffn2_norm_pre[l], scale(2), shift(2)), res_w=1.0)
        y = _ffn(h.reshape(bsz * s, d), ffn2_w_gate[l], ffn2_w_up[l], ffn2_w_down[l])
        x = _norm_mod(x, post=(y.reshape(bsz, s, d), ffn2_norm_post[l], gate(2)), res_w=0.5)[0]
    return x
```

```python
import functools
import math

import jax
import jax.numpy as jnp
from jax import lax
from jax.experimental import pallas as pl
from jax.experimental.pallas import tpu as pltpu

D_MODEL = 4096
HEAD_DIM = 128
N_HEADS_A = 16
N_KV_A = 2
WINDOW = 128
N_HEADS_B = 16
N_KV_B = 4
MOBA_BLOCK = 256
MOBA_TOPK = 3
D_FF = 11008
EPS = 1e-6
NEG_INF = -1e30
N_MOD = 9

WIDTH_A = N_HEADS_A * HEAD_DIM
WIDTH_B = N_HEADS_B * HEAD_DIM
KV_A = N_KV_A * HEAD_DIM
KV_B = N_KV_B * HEAD_DIM
D_IN = WIDTH_A + 2 * KV_A + WIDTH_B + 2 * KV_B

OFF_QA = 0
OFF_KA = OFF_QA + WIDTH_A
OFF_VA = OFF_KA + KV_A
OFF_QB = OFF_VA + KV_A
OFF_KB = OFF_QB + WIDTH_B
OFF_VB = OFF_KB + KV_B

MIB = 1024 * 1024
F32 = jnp.float32
BF16 = jnp.bfloat16
NT_DIMS = (((1,), (1,)), ((), ()))


def _params(n_grid, vmem_mib):
    return pltpu.CompilerParams(dimension_semantics=("arbitrary",) * n_grid,
                                vmem_limit_bytes=vmem_mib * MIB)


def _rms(v, g):
    return v * lax.rsqrt(jnp.mean(v * v, axis=-1, keepdims=True) + EPS) * g


def _ada_kernel(c_ref, w_ref, b_ref, o_ref):
    c = c_ref[...]
    s = c * jax.nn.sigmoid(c)
    hi = s.astype(BF16)
    lo = (s - hi.astype(F32)).astype(BF16)
    lhs = jnp.concatenate([hi, lo], axis=0)
    r = jnp.dot(lhs, w_ref[...].astype(BF16), preferred_element_type=F32)
    o_ref[...] = r[:8] + r[8:] + b_ref[...]


def _ada(c, w, b):
    bsz, d = c.shape
    n = w.shape[1]
    tn = 1024
    c8 = jnp.pad(c, ((0, 8 - bsz), (0, 0)))
    out = pl.pallas_call(
        _ada_kernel,
        grid=(n // tn,),
        in_specs=[pl.BlockSpec((8, d), lambda j: (0, 0)),
                  pl.BlockSpec((d, tn), lambda j: (0, j)),
                  pl.BlockSpec((1, tn), lambda j: (0, j))],
        out_specs=pl.BlockSpec((8, tn), lambda j: (0, j)),
        out_shape=jax.ShapeDtypeStruct((8, n), F32),
        compiler_params=_params(1, 48),
        name="ada",
    )(c8, w, b.reshape(1, n))
    return out[:bsz]


def _norm_mod_kernel(*refs, has_post, has_pre, res_w):
    refs = list(refs)
    x = refs.pop(0)[...]
    if has_post:
        y_ref, gpost_ref, gate_ref = refs.pop(0), refs.pop(0), refs.pop(0)
    if has_pre:
        gpre_ref, scale_ref, shift_ref = refs.pop(0), refs.pop(0), refs.pop(0)
    if has_post:
        x = x + res_w * gate_ref[...] * _rms(y_ref[...].astype(F32), gpost_ref[...])
        refs.pop(0)[...] = x
    if has_pre:
        refs.pop(0)[...] = (_rms(x, gpre_ref[...]) * (1.0 + scale_ref[...]) + shift_ref[...]).astype(BF16)


def _norm_mod(x, post=None, pre=None, res_w=1.0):
    bsz, s, d = x.shape
    ts = 256
    row = pl.BlockSpec((None, ts, d), lambda b, t: (b, t, 0))
    gain = pl.BlockSpec((1, d), lambda b, t: (0, 0))
    mod = pl.BlockSpec((None, 1, d), lambda b, t: (b, 0, 0))
    args, in_specs, out_specs, out_shape = [x], [row], [], []
    if post is not None:
        y, g_post, gate = post
        args += [y, g_post.reshape(1, d), gate]
        in_specs += [row, gain, mod]
        out_specs.append(row)
        out_shape.append(jax.ShapeDtypeStruct((bsz, s, d), F32))
    if pre is not None:
        g_pre, scale, shift = pre
        args += [g_pre.reshape(1, d), scale, shift]
        in_specs += [gain, mod, mod]
        out_specs.append(row)
        out_shape.append(jax.ShapeDtypeStruct((bsz, s, d), BF16))
    return pl.pallas_call(
        functools.partial(_norm_mod_kernel, has_post=post is not None, has_pre=pre is not None, res_w=res_w),
        grid=(bsz, s // ts),
        in_specs=in_specs, out_specs=out_specs, out_shape=out_shape,
        compiler_params=_params(2, 48),
        name="norm_mod",
    )(*args)


def _gate_up_kernel(h_ref, wg_ref, wu_ref, o_ref):
    h = h_ref[...]
    g = jnp.dot(h, wg_ref[...].astype(BF16), preferred_element_type=F32)
    u = jnp.dot(h, wu_ref[...].astype(BF16), preferred_element_type=F32)
    o_ref[...] = (g * jax.nn.sigmoid(g) * u).astype(o_ref.dtype)


def _gate_up(h, wg, wu):
    m, k = h.shape
    f = wg.shape[1]
    tm, tf = 2048, 256
    return pl.pallas_call(
        _gate_up_kernel,
        grid=(m // tm, f // tf),
        in_specs=[pl.BlockSpec((tm, k), lambda i, j: (i, 0), pipeline_mode=pl.Buffered(1)),
                  pl.BlockSpec((k, tf), lambda i, j: (0, j)),
                  pl.BlockSpec((k, tf), lambda i, j: (0, j))],
        out_specs=pl.BlockSpec((tm, tf), lambda i, j: (i, j)),
        out_shape=jax.ShapeDtypeStruct((m, f), BF16),
        compiler_params=_params(2, 56),
        name="gate_up",
    )(h, wg, wu)


def _mm_kernel(*refs, splits):
    w_ref, o_ref = refs[len(splits)], refs[-1]
    acc, off = None, 0
    for a_ref, k in zip(refs, splits):
        part = jnp.dot(a_ref[...], w_ref[off:off + k, :], preferred_element_type=F32)
        acc = part if acc is None else acc + part
        off += k
    o_ref[...] = acc.astype(o_ref.dtype)


def _matmul(a_parts, w, tm, tn, vmem_mib, name):
    m = a_parts[0].shape[0]
    n = w.shape[1]
    splits = tuple(a.shape[1] for a in a_parts)
    return pl.pallas_call(
        functools.partial(_mm_kernel, splits=splits),
        grid=(m // tm, n // tn),
        in_specs=[pl.BlockSpec((tm, k), lambda i, j: (i, 0)) for k in splits]
        + [pl.BlockSpec((sum(splits), tn), lambda i, j: (0, j))],
        out_specs=pl.BlockSpec((tm, tn), lambda i, j: (i, j)),
        out_shape=jax.ShapeDtypeStruct((m, n), BF16),
        compiler_params=_params(2, vmem_mib),
        name=name,
    )(*a_parts, w)


SWA_TQ = 512


def _swa_kernel(sink_ref, slope_ref, q_ref, kown_ref, vown_ref, kprev_ref, vprev_ref, g_ref, o_ref, o_scr):
    t = pl.program_id(1)
    group = N_HEADS_A // N_KV_A
    scale = HEAD_DIM ** -0.5
    kj = lax.broadcasted_iota(jnp.int32, (WINDOW, 2 * WINDOW), 1)
    dist = lax.broadcasted_iota(jnp.int32, (WINDOW, 2 * WINDOW), 0) + WINDOW - kj
    in_window = (dist >= 0) & (dist < WINDOW)
    distf = dist.astype(F32)
    first_key = jnp.where(t > 0, 0, WINDOW)
    for s in range(SWA_TQ // WINDOW):
        rows = slice(s * WINDOW, (s + 1) * WINDOW)
        if s == 0:
            kk = jnp.concatenate([kprev_ref[...], kown_ref[0:WINDOW, :]], axis=0)
            vv = jnp.concatenate([vprev_ref[...], vown_ref[0:WINDOW, :]], axis=0)
            valid = in_window & (kj >= first_key)
        else:
            kk = kown_ref[(s - 1) * WINDOW:(s + 1) * WINDOW, :]
            vv = vown_ref[(s - 1) * WINDOW:(s + 1) * WINDOW, :]
            valid = in_window
        for h in range(N_HEADS_A):
            kv = h // group
            cols = slice(h * HEAD_DIM, (h + 1) * HEAD_DIM)
            kvc = slice(kv * HEAD_DIM, (kv + 1) * HEAD_DIM)
            sc = lax.dot_general(q_ref[rows, cols], kk[:, kvc], NT_DIMS, preferred_element_type=F32) * scale
            sc = jnp.where(valid, sc - slope_ref[h] * distf, NEG_INF)
            sink = sink_ref[h]
            mx = jnp.maximum(jnp.max(sc, axis=-1, keepdims=True), sink)
            p = jnp.exp(sc - mx)
            denom = jnp.sum(p, axis=-1, keepdims=True) + jnp.exp(sink - mx)
            o_scr[rows, cols] = jnp.dot((p / denom).astype(BF16), vv[:, kvc], preferred_element_type=F32)
    o_ref[...] = _rms(o_scr[...], g_ref[...]).astype(o_ref.dtype)


def _swa(proj, sinks, slopes, norm_a):
    bsz, s, _ = proj.shape
    wpt = SWA_TQ // WINDOW
    smem = pl.BlockSpec(memory_space=pltpu.SMEM)
    own = lambda col: (lambda b, t: (b, t, col))
    prev = lambda col: (lambda b, t: (b, jnp.maximum(t * wpt - 1, 0), col))
    return pl.pallas_call(
        _swa_kernel,
        grid=(bsz, s // SWA_TQ),
        in_specs=[smem, smem,
                  pl.BlockSpec((None, SWA_TQ, WIDTH_A), own(OFF_QA // WIDTH_A)),
                  pl.BlockSpec((None, SWA_TQ, KV_A), own(OFF_KA // KV_A)),
                  pl.BlockSpec((None, SWA_TQ, KV_A), own(OFF_VA // KV_A)),
                  pl.BlockSpec((None, WINDOW, KV_A), prev(OFF_KA // KV_A)),
                  pl.BlockSpec((None, WINDOW, KV_A), prev(OFF_VA // KV_A)),
                  pl.BlockSpec((1, WIDTH_A), lambda b, t: (0, 0))],
        out_specs=pl.BlockSpec((None, SWA_TQ, WIDTH_A), lambda b, t: (b, t, 0)),
        out_shape=jax.ShapeDtypeStruct((bsz, s, WIDTH_A), BF16),
        scratch_shapes=[pltpu.VMEM((SWA_TQ, WIDTH_A), F32)],
        compiler_params=_params(2, 40),
        name="swa",
    )(sinks, slopes, proj, proj, proj, proj, proj, norm_a.reshape(1, WIDTH_A))


MOBA_GROUP = N_HEADS_B // N_KV_B
N_ALIBI_TERMS = 3


def _moba_kernel(slope_ref, q0_ref, q1_ref, q2_ref, q3_ref, k_ref, vt_ref, g_ref, o_ref,
                 kmean_scr, qext_scr, bias_scr, m_scr, l_scr, acc_scr, y_scr):
    q_refs = (q0_ref, q1_ref, q2_ref, q3_ref)
    group_cols = [slice(g * HEAD_DIM, (g + 1) * HEAD_DIM) for g in range(MOBA_GROUP)]
    n = pl.program_id(1)
    blk_sz = MOBA_BLOCK
    nblk = k_ref.shape[0] // blk_sz
    qrows = MOBA_GROUP * blk_sz
    scale = HEAD_DIM ** -0.5
    to_log2 = scale * math.log2(math.e)

    @pl.when(n == 0)
    def _():
        kmean = jnp.concatenate(
            [jnp.sum(k_ref[i * blk_sz:(i + 1) * blk_sz, :].astype(F32), axis=0, keepdims=True)
             for i in range(nblk)], axis=0) * (1.0 / blk_sz)
        hi = kmean.astype(BF16)
        lo = (kmean - hi.astype(F32)).astype(BF16)
        for j in range(N_KV_B):
            kc = slice(j * HEAD_DIM, (j + 1) * HEAD_DIM)
            kmean_scr[j] = jnp.concatenate([hi[:, kc], lo[:, kc]], axis=0)
        f_lane = lax.broadcasted_iota(jnp.int32, (blk_sz, HEAD_DIM), 1)
        for h in range(N_HEADS_B):
            a = jnp.full((blk_sz, HEAD_DIM), slope_ref[h], F32) * (1.0 / scale)
            a1 = a.astype(BF16).astype(F32)
            a2 = (a - a1).astype(BF16).astype(F32)
            a3 = (a - a1 - a2).astype(BF16).astype(F32)
            term = jnp.where((f_lane == 0) | (f_lane == 3), a1, jnp.where((f_lane == 1) | (f_lane == 4), a2, a3))
            qext_scr[h, :, HEAD_DIM:2 * HEAD_DIM] = jnp.where(f_lane < 2 * N_ALIBI_TERMS, term, 0.0).astype(BF16)

    lane = lax.broadcasted_iota(jnp.int32, (blk_sz, HEAD_DIM), 1)
    key_off = lax.broadcasted_iota(jnp.int32, (blk_sz, HEAD_DIM), 0).astype(F32)
    off_lanes = lane < N_ALIBI_TERMS
    far_lanes = (lane >= N_ALIBI_TERMS) & (lane < 2 * N_ALIBI_TERMS)
    own_feat = jnp.where(off_lanes, key_off, 0.0).astype(BF16)
    causal = jnp.where(lax.broadcasted_iota(jnp.int32, (blk_sz, blk_sz), 0)
                       <= lax.broadcasted_iota(jnp.int32, (blk_sz, blk_sz), 1), 0.0, NEG_INF)
    blk = lax.broadcasted_iota(jnp.int32, (nblk, qrows), 0)

    head_cols = [slice(h * HEAD_DIM, (h + 1) * HEAD_DIM) for h in range(N_HEADS_B)]
    kv_cols = [slice(j * HEAD_DIM, (j + 1) * HEAD_DIM) for j in range(N_KV_B)]

    for j in range(N_KV_B):
        qs = jnp.concatenate([q_refs[j][:, gc] for gc in group_cols], axis=0)
        g2 = lax.dot_general(kmean_scr[j], qs, NT_DIMS, preferred_element_type=F32)
        gate = g2[:nblk] + g2[nblk:]
        beaten = jnp.zeros((nblk, qrows), F32)
        for i in range(nblk):
            gi = gate[i:i + 1, :]
            beats = (gi > gate) | ((gi == gate) & (blk > i))
            beaten = beaten + jnp.where(beats, jnp.where(i < n, 1.0, 0.0), 0.0)
        bias_t = jnp.where((blk < n) & (beaten < MOBA_TOPK), 0.0, NEG_INF)

        for g in range(MOBA_GROUP):
            h = j * MOBA_GROUP + g
            bias_scr[h] = bias_t[:, g * blk_sz:(g + 1) * blk_sz]
            qext_scr[h, :, 0:HEAD_DIM] = q_refs[j][:, group_cols[g]]

    def attend(i, own):
        st = pl.multiple_of(i * blk_sz, blk_sz)
        if own:
            kfeat = own_feat
        else:
            far = ((n - i) * blk_sz).astype(F32)
            kfeat = jnp.where(far_lanes, -far, jnp.where(off_lanes, key_off, 0.0)).astype(BF16)
        logits = []
        for j in range(N_KV_B):
            k_ext = jnp.concatenate([k_ref[pl.ds(st, blk_sz), kv_cols[j]], kfeat], axis=1)
            for g in range(MOBA_GROUP):
                logits.append(lax.dot_general(k_ext, qext_scr[j * MOBA_GROUP + g], NT_DIMS,
                                              preferred_element_type=F32))
        for h in range(N_HEADS_B):
            t = logits[h] * to_log2
            if own:
                t = t + causal
                m_new = jnp.max(t, axis=0, keepdims=True)
            else:
                t = t + bias_scr[h, pl.ds(i, 1), :]
                m_old = m_scr[h][0:1, :]
                m_new = jnp.maximum(m_old, jnp.max(t, axis=0, keepdims=True))
            p = jnp.exp2(t - m_new)
            l_new = jnp.sum(p, axis=0, keepdims=True)
            acc = jnp.dot(vt_ref[i, kv_cols[h // MOBA_GROUP], :], p.astype(BF16), preferred_element_type=F32)
            if not own:
                alpha = jnp.exp2(m_old - m_new)
                l_new = alpha * l_scr[h][0:1, :] + l_new
                acc = alpha * acc_scr[h] + acc
            m_scr[h] = jnp.broadcast_to(m_new, (8, blk_sz))
            l_scr[h] = jnp.broadcast_to(l_new, (8, blk_sz))
            acc_scr[h] = acc

    attend(n, own=True)

    def past_block(i, carry):
        attend(i, own=False)
        return carry

    lax.fori_loop(0, n, past_block, 0)

    for h in range(N_HEADS_B):
        y_scr[:, head_cols[h]] = (acc_scr[h] / l_scr[h][0:1, :]).T
    o_ref[...] = _rms(y_scr[...], g_ref[...]).astype(o_ref.dtype)


def _moba(proj, slopes, norm_b):
    bsz, s, _ = proj.shape
    nblk = s // MOBA_BLOCK
    v_t = proj[:, :, OFF_VB:OFF_VB + KV_B].reshape(bsz, nblk, MOBA_BLOCK, KV_B).transpose(0, 1, 3, 2)
    q_width = MOBA_GROUP * HEAD_DIM

    def q_block(j, b, n):
        return (b, n, OFF_QB // q_width + j)

    return pl.pallas_call(
        _moba_kernel,
        grid=(bsz, nblk),
        in_specs=[pl.BlockSpec(memory_space=pltpu.SMEM),
                  *[pl.BlockSpec((None, MOBA_BLOCK, q_width), functools.partial(q_block, j))
                    for j in range(N_KV_B)],
                  pl.BlockSpec((None, s, KV_B), lambda b, n: (b, 0, OFF_KB // KV_B)),
                  pl.BlockSpec((None, nblk, KV_B, MOBA_BLOCK), lambda b, n: (b, 0, 0, 0)),
                  pl.BlockSpec((1, WIDTH_B), lambda b, n: (0, 0))],
        out_specs=pl.BlockSpec((None, MOBA_BLOCK, WIDTH_B), lambda b, n: (b, n, 0)),
        out_shape=jax.ShapeDtypeStruct((bsz, s, WIDTH_B), BF16),
        scratch_shapes=[pltpu.VMEM((N_KV_B, 2 * nblk, HEAD_DIM), BF16),
                        pltpu.VMEM((N_HEADS_B, MOBA_BLOCK, 2 * HEAD_DIM), BF16),
                        pltpu.VMEM((N_HEADS_B, nblk, MOBA_BLOCK), F32),
                        pltpu.VMEM((N_HEADS_B, 8, MOBA_BLOCK), F32),
                        pltpu.VMEM((N_HEADS_B, 8, MOBA_BLOCK), F32),
                        pltpu.VMEM((N_HEADS_B, HEAD_DIM, MOBA_BLOCK), F32),
                        pltpu.VMEM((MOBA_BLOCK, WIDTH_B), F32)],
        compiler_params=_params(2, 40),
        name="moba",
    )(slopes, *([proj] * N_KV_B), proj, v_t, norm_b.reshape(1, WIDTH_B))


def _alibi_slopes(n):
    return 2.0 ** (-8.0 * jnp.arange(1, n + 1, dtype=F32) / n)


def _ffn(h, w_gate, w_up, w_down):
    act = _gate_up(h, w_gate, w_up)
    return _matmul([act], w_down.astype(BF16), tm=512, tn=512, vmem_mib=58, name="ffn_down")


def kernel(x, c, w_ada, b_ada, ffn1_norm_pre, ffn1_norm_post, ffn1_w_gate, ffn1_w_up, ffn1_w_down, mix_norm_pre, mix_norm_post, w_in, attn_sinks, norm_a, norm_b, w_out, ffn2_norm_pre, ffn2_norm_post, ffn2_w_gate, ffn2_w_up, ffn2_w_down):
    bsz, s, d = x.shape
    depth = w_ada.shape[0]
    slopes_a = _alibi_slopes(N_HEADS_A)
    slopes_b = _alibi_slopes(N_HEADS_B)
    for l in range(depth):
        mod = _ada(c, w_ada[l], b_ada[l]).reshape(bsz, N_MOD, 1, d)
        shift, scale, gate = (lambda i: mod[:, 3 * i]), (lambda i: mod[:, 3 * i + 1]), (lambda i: mod[:, 3 * i + 2])

        h = _norm_mod(x, pre=(ffn1_norm_pre[l], scale(0), shift(0)))[0]
        y = _ffn(h.reshape(bsz * s, d), ffn1_w_gate[l], ffn1_w_up[l], ffn1_w_down[l])

        x, h = _norm_mod(x, post=(y.reshape(bsz, s, d), ffn1_norm_post[l], gate(0)),
                         pre=(mix_norm_pre[l], scale(1), shift(1)), res_w=0.5)
        proj = _matmul([h.reshape(bsz * s, d)], w_in[l].astype(BF16), tm=2048, tn=512, vmem_mib=48, name="w_in")
        proj = proj.reshape(bsz, s, D_IN)
        ya = _swa(proj, attn_sinks[l], slopes_a, norm_a[l])
        yb = _moba(proj, slopes_b, norm_b[l])
        y = _matmul([ya.reshape(bsz * s, WIDTH_A), yb.reshape(bsz * s, WIDTH_B)], w_out[l].astype(BF16),
                    tm=2048, tn=512, vmem_mib=48, name="w_out")

        x, h = _norm_mod(x, post=(y.reshape(bsz, s, d), mix_norm_post[l], gate(1)),
                         pre=(ffn2_norm_pre[l], scale(2), shift(2)), res_w=1.0)
        y = _ffn(h.reshape(bsz * s, d), ffn2_w_gate[l], ffn2_w_up[l], ffn2_w_down[l])
        x = _norm_mod(x, post=(y.reshape(bsz, s, d), ffn2_norm_post[l], gate(2)), res_w=0.5)[0]
    return x
```

```python
import functools
import math

import jax
import jax.numpy as jnp
from jax import lax
from jax.experimental import pallas as pl
from jax.experimental.pallas import tpu as pltpu

D_MODEL = 4096
HEAD_DIM = 128
N_HEADS_A = 16
N_KV_A = 2
WINDOW = 128
N_HEADS_B = 16
N_KV_B = 4
MOBA_BLOCK = 256
MOBA_TOPK = 3
D_FF = 11008
EPS = 1e-6
NEG_INF = -1e30
N_MOD = 9

WIDTH_A = N_HEADS_A * HEAD_DIM
WIDTH_B = N_HEADS_B * HEAD_DIM
KV_A = N_KV_A * HEAD_DIM
KV_B = N_KV_B * HEAD_DIM
D_IN = WIDTH_A + 2 * KV_A + WIDTH_B + 2 * KV_B

OFF_QA = 0
OFF_KA = OFF_QA + WIDTH_A
OFF_VA = OFF_KA + KV_A
OFF_QB = OFF_VA + KV_A
OFF_KB = OFF_QB + WIDTH_B
OFF_VB = OFF_KB + KV_B

MIB = 1024 * 1024
F32 = jnp.float32
BF16 = jnp.bfloat16
NT_DIMS = (((1,), (1,)), ((), ()))


def _params(n_grid, vmem_mib):
    return pltpu.CompilerParams(dimension_semantics=("arbitrary",) * n_grid,
                                vmem_limit_bytes=vmem_mib * MIB)


def _rms(v, g):
    return v * lax.rsqrt(jnp.mean(v * v, axis=-1, keepdims=True) + EPS) * g


def _ada_kernel(c_ref, w_ref, b_ref, o_ref):
    c = c_ref[...]
    s = c * jax.nn.sigmoid(c)
    hi = s.astype(BF16)
    lo = (s - hi.astype(F32)).astype(BF16)
    lhs = jnp.concatenate([hi, lo], axis=0)
    r = jnp.dot(lhs, w_ref[...].astype(BF16), preferred_element_type=F32)
    o_ref[...] = r[:8] + r[8:] + b_ref[...]


def _ada(c, w, b):
    bsz, d = c.shape
    n = w.shape[1]
    tn = 1024
    c8 = jnp.pad(c, ((0, 8 - bsz), (0, 0)))
    out = pl.pallas_call(
        _ada_kernel,
        grid=(n // tn,),
        in_specs=[pl.BlockSpec((8, d), lambda j: (0, 0)),
                  pl.BlockSpec((d, tn), lambda j: (0, j)),
                  pl.BlockSpec((1, tn), lambda j: (0, j))],
        out_specs=pl.BlockSpec((8, tn), lambda j: (0, j)),
        out_shape=jax.ShapeDtypeStruct((8, n), F32),
        compiler_params=_params(1, 48),
        name="ada",
    )(c8, w, b.reshape(1, n))
    return out[:bsz]


def _norm_mod_kernel(*refs, has_post, has_pre, res_w):
    refs = list(refs)
    x = refs.pop(0)[...]
    if has_post:
        y_ref, gpost_ref, gate_ref = refs.pop(0), refs.pop(0), refs.pop(0)
    if has_pre:
        gpre_ref, scale_ref, shift_ref = refs.pop(0), refs.pop(0), refs.pop(0)
    if has_post:
        x = x + res_w * gate_ref[...] * _rms(y_ref[...].astype(F32), gpost_ref[...])
        refs.pop(0)[...] = x
    if has_pre:
        refs.pop(0)[...] = (_rms(x, gpre_ref[...]) * (1.0 + scale_ref[...]) + shift_ref[...]).astype(BF16)


def _norm_mod(x, post=None, pre=None, res_w=1.0):
    bsz, s, d = x.shape
    ts = 256
    row = pl.BlockSpec((None, ts, d), lambda b, t: (b, t, 0))
    gain = pl.BlockSpec((1, d), lambda b, t: (0, 0))
    mod = pl.BlockSpec((None, 1, d), lambda b, t: (b, 0, 0))
    args, in_specs, out_specs, out_shape = [x], [row], [], []
    if post is not None:
        y, g_post, gate = post
        args += [y, g_post.reshape(1, d), gate]
        in_specs += [row, gain, mod]
        out_specs.append(row)
        out_shape.append(jax.ShapeDtypeStruct((bsz, s, d), F32))
    if pre is not None:
        g_pre, scale, shift = pre
        args += [g_pre.reshape(1, d), scale, shift]
        in_specs += [gain, mod, mod]
        out_specs.append(row)
        out_shape.append(jax.ShapeDtypeStruct((bsz, s, d), BF16))
    return pl.pallas_call(
        functools.partial(_norm_mod_kernel, has_post=post is not None, has_pre=pre is not None, res_w=res_w),
        grid=(bsz, s // ts),
        in_specs=in_specs, out_specs=out_specs, out_shape=out_shape,
        compiler_params=_params(2, 48),
        name="norm_mod",
    )(*args)


def _gate_up_kernel(h_ref, wg_ref, wu_ref, o_ref):
    h = h_ref[...]
    g = jnp.dot(h, wg_ref[...].astype(BF16), preferred_element_type=F32)
    u = jnp.dot(h, wu_ref[...].astype(BF16), preferred_element_type=F32)
    o_ref[...] = (g * jax.nn.sigmoid(g) * u).astype(o_ref.dtype)


def _gate_up(h, wg, wu):
    m, k = h.shape
    f = wg.shape[1]
    tm, tf = 2048, 256
    return pl.pallas_call(
        _gate_up_kernel,
        grid=(m // tm, f // tf),
        in_specs=[pl.BlockSpec((tm, k), lambda i, j: (i, 0)),
                  pl.BlockSpec((k, tf), lambda i, j: (0, j)),
                  pl.BlockSpec((k, tf), lambda i, j: (0, j))],
        out_specs=pl.BlockSpec((tm, tf), lambda i, j: (i, j)),
        out_shape=jax.ShapeDtypeStruct((m, f), BF16),
        compiler_params=_params(2, 58),
        name="gate_up",
    )(h, wg, wu)


def _mm_kernel(*refs, splits):
    w_ref, o_ref = refs[len(splits)], refs[-1]
    acc, off = None, 0
    for a_ref, k in zip(refs, splits):
        part = jnp.dot(a_ref[...], w_ref[off:off + k, :], preferred_element_type=F32)
        acc = part if acc is None else acc + part
        off += k
    o_ref[...] = acc.astype(o_ref.dtype)


def _matmul(a_parts, w, tm, tn, vmem_mib, name):
    m = a_parts[0].shape[0]
    n = w.shape[1]
    splits = tuple(a.shape[1] for a in a_parts)
    return pl.pallas_call(
        functools.partial(_mm_kernel, splits=splits),
        grid=(m // tm, n // tn),
        in_specs=[pl.BlockSpec((tm, k), lambda i, j: (i, 0)) for k in splits]
        + [pl.BlockSpec((sum(splits), tn), lambda i, j: (0, j))],
        out_specs=pl.BlockSpec((tm, tn), lambda i, j: (i, j)),
        out_shape=jax.ShapeDtypeStruct((m, n), BF16),
        compiler_params=_params(2, vmem_mib),
        name=name,
    )(*a_parts, w)


SWA_TQ = 512
SWA_GROUP = N_HEADS_A // N_KV_A


def _swa_kernel(sink_ref, slope_ref, q_ref, kown_ref, kprev_ref, vtown_ref, vtprev_ref, g_ref, o_ref,
                bias_scr, o_scr):
    b, t = pl.program_id(0), pl.program_id(1)
    to_log2 = (HEAD_DIM ** -0.5) * math.log2(math.e)
    keys, lanes = 2 * WINDOW, SWA_GROUP * WINDOW

    @pl.when((b == 0) & (t == 0))
    def _():
        key = lax.broadcasted_iota(jnp.int32, (keys, lanes), 0)
        qry = lax.broadcasted_iota(jnp.int32, (keys, lanes), 1) & (WINDOW - 1)
        dist = qry + WINDOW - key
        inside = (dist >= 0) & (dist < WINDOW)
        distf = dist.astype(F32) * math.log2(math.e)
        for j in range(N_KV_A):
            slope = jnp.concatenate(
                [jnp.full((keys, WINDOW), slope_ref[j * SWA_GROUP + g], F32) for g in range(SWA_GROUP)], axis=1)
            bias_scr[j] = jnp.where(inside, -(slope * distf), NEG_INF)

    first_key = jnp.where(t > 0, 0, WINDOW)
    no_prev = jnp.where(lax.broadcasted_iota(jnp.int32, (keys, lanes), 0) >= first_key, 0.0, NEG_INF)

    for j in range(N_KV_A):
        kvc = slice(j * HEAD_DIM, (j + 1) * HEAD_DIM)
        sink2 = jnp.concatenate(
            [jnp.full((1, WINDOW), sink_ref[j * SWA_GROUP + g], F32) for g in range(SWA_GROUP)], axis=1
        ) * math.log2(math.e)
        for s in range(SWA_TQ // WINDOW):
            rows = slice(s * WINDOW, (s + 1) * WINDOW)
            if s == 0:
                kk = jnp.concatenate([kprev_ref[:, kvc], kown_ref[0:WINDOW, kvc]], axis=0)
                vv_t = jnp.concatenate([vtprev_ref[kvc, :], vtown_ref[kvc, 0:WINDOW]], axis=1)
            else:
                kk = kown_ref[(s - 1) * WINDOW:(s + 1) * WINDOW, kvc]
                vv_t = vtown_ref[kvc, (s - 1) * WINDOW:(s + 1) * WINDOW]
            qs = jnp.concatenate(
                [q_ref[rows, (j * SWA_GROUP + g) * HEAD_DIM:(j * SWA_GROUP + g + 1) * HEAD_DIM]
                 for g in range(SWA_GROUP)], axis=0)
            sc = lax.dot_general(kk, qs, NT_DIMS, preferred_element_type=F32) * to_log2 + bias_scr[j]
            if s == 0:
                sc = sc + no_prev
            mx = jnp.maximum(jnp.max(sc, axis=0, keepdims=True), sink2)
            p = jnp.exp2(sc - mx)
            denom = jnp.sum(p, axis=0, keepdims=True) + jnp.exp2(sink2 - mx)
            o_t = jnp.dot(vv_t, p.astype(BF16), preferred_element_type=F32) / denom
            for g in range(SWA_GROUP):
                h = j * SWA_GROUP + g
                o_scr[rows, h * HEAD_DIM:(h + 1) * HEAD_DIM] = o_t[:, g * WINDOW:(g + 1) * WINDOW].T
    o_ref[...] = _rms(o_scr[...], g_ref[...]).astype(o_ref.dtype)


def _swa(proj, sinks, slopes, norm_a):
    bsz, s, _ = proj.shape
    wpt = SWA_TQ // WINDOW
    v_t = proj[:, :, OFF_VA:OFF_VA + KV_A].transpose(0, 2, 1)
    smem = pl.BlockSpec(memory_space=pltpu.SMEM)
    return pl.pallas_call(
        _swa_kernel,
        grid=(bsz, s // SWA_TQ),
        in_specs=[smem, smem,
                  pl.BlockSpec((None, SWA_TQ, WIDTH_A), lambda b, t: (b, t, OFF_QA // WIDTH_A)),
                  pl.BlockSpec((None, SWA_TQ, KV_A), lambda b, t: (b, t, OFF_KA // KV_A)),
                  pl.BlockSpec((None, WINDOW, KV_A), lambda b, t: (b, jnp.maximum(t * wpt - 1, 0), OFF_KA // KV_A)),
                  pl.BlockSpec((None, KV_A, SWA_TQ), lambda b, t: (b, 0, t)),
                  pl.BlockSpec((None, KV_A, WINDOW), lambda b, t: (b, 0, jnp.maximum(t * wpt - 1, 0))),
                  pl.BlockSpec((1, WIDTH_A), lambda b, t: (0, 0))],
        out_specs=pl.BlockSpec((None, SWA_TQ, WIDTH_A), lambda b, t: (b, t, 0)),
        out_shape=jax.ShapeDtypeStruct((bsz, s, WIDTH_A), BF16),
        scratch_shapes=[pltpu.VMEM((N_KV_A, 2 * WINDOW, SWA_GROUP * WINDOW), F32),
                        pltpu.VMEM((SWA_TQ, WIDTH_A), F32)],
        compiler_params=_params(2, 40),
        name="swa",
    )(sinks, slopes, proj, proj, proj, v_t, v_t, norm_a.reshape(1, WIDTH_A))


MOBA_GROUP = N_HEADS_B // N_KV_B
N_ALIBI_TERMS = 3


def _moba_kernel(slope_ref, q0_ref, q1_ref, q2_ref, q3_ref, k_ref, vt_ref, g_ref, o_ref,
                 kmean_scr, qext_scr, bias_scr, m_scr, l_scr, acc_scr, y_scr):
    q_refs = (q0_ref, q1_ref, q2_ref, q3_ref)
    group_cols = [slice(g * HEAD_DIM, (g + 1) * HEAD_DIM) for g in range(MOBA_GROUP)]
    n = pl.program_id(1)
    blk_sz = MOBA_BLOCK
    nblk = k_ref.shape[0] // blk_sz
    qrows = MOBA_GROUP * blk_sz
    scale = HEAD_DIM ** -0.5
    to_log2 = scale * math.log2(math.e)

    @pl.when(n == 0)
    def _():
        kmean = jnp.concatenate(
            [jnp.sum(k_ref[i * blk_sz:(i + 1) * blk_sz, :].astype(F32), axis=0, keepdims=True)
             for i in range(nblk)], axis=0) * (1.0 / blk_sz)
        hi = kmean.astype(BF16)
        lo = (kmean - hi.astype(F32)).astype(BF16)
        for j in range(N_KV_B):
            kc = slice(j * HEAD_DIM, (j + 1) * HEAD_DIM)
            kmean_scr[j] = jnp.concatenate([hi[:, kc], lo[:, kc]], axis=0)
        f_lane = lax.broadcasted_iota(jnp.int32, (blk_sz, HEAD_DIM), 1)
        for h in range(N_HEADS_B):
            a = jnp.full((blk_sz, HEAD_DIM), slope_ref[h], F32) * (1.0 / scale)
            a1 = a.astype(BF16).astype(F32)
            a2 = (a - a1).astype(BF16).astype(F32)
            a3 = (a - a1 - a2).astype(BF16).astype(F32)
            term = jnp.where((f_lane == 0) | (f_lane == 3), a1, jnp.where((f_lane == 1) | (f_lane == 4), a2, a3))
            qext_scr[h, :, HEAD_DIM:2 * HEAD_DIM] = jnp.where(f_lane < 2 * N_ALIBI_TERMS, term, 0.0).astype(BF16)

    lane = lax.broadcasted_iota(jnp.int32, (blk_sz, HEAD_DIM), 1)
    key_off = lax.broadcasted_iota(jnp.int32, (blk_sz, HEAD_DIM), 0).astype(F32)
    off_lanes = lane < N_ALIBI_TERMS
    far_lanes = (lane >= N_ALIBI_TERMS) & (lane < 2 * N_ALIBI_TERMS)
    own_feat = jnp.where(off_lanes, key_off, 0.0).astype(BF16)
    causal = jnp.where(lax.broadcasted_iota(jnp.int32, (blk_sz, blk_sz), 0)
                       <= lax.broadcasted_iota(jnp.int32, (blk_sz, blk_sz), 1), 0.0, NEG_INF)
    blk = lax.broadcasted_iota(jnp.int32, (nblk, qrows), 0)

    head_cols = [slice(h * HEAD_DIM, (h + 1) * HEAD_DIM) for h in range(N_HEADS_B)]
    kv_cols = [slice(j * HEAD_DIM, (j + 1) * HEAD_DIM) for j in range(N_KV_B)]

    for j in range(N_KV_B):
        qs = jnp.concatenate([q_refs[j][:, gc] for gc in group_cols], axis=0)
        g2 = lax.dot_general(kmean_scr[j], qs, NT_DIMS, preferred_element_type=F32)
        gate = g2[:nblk] + g2[nblk:]
        beaten = jnp.zeros((nblk, qrows), F32)
        for i in range(nblk):
            gi = gate[i:i + 1, :]
            beats = (gi > gate) | ((gi == gate) & (blk > i))
            beaten = beaten + jnp.where(beats, jnp.where(i < n, 1.0, 0.0), 0.0)
        bias_t = jnp.where((blk < n) & (beaten < MOBA_TOPK), 0.0, NEG_INF)

        for g in range(MOBA_GROUP):
            h = j * MOBA_GROUP + g
            bias_scr[h] = bias_t[:, g * blk_sz:(g + 1) * blk_sz]
            qext_scr[h, :, 0:HEAD_DIM] = q_refs[j][:, group_cols[g]]

    def attend(i, own):
        st = pl.multiple_of(i * blk_sz, blk_sz)
        if own:
            kfeat = own_feat
        else:
            far = ((n - i) * blk_sz).astype(F32)
            kfeat = jnp.where(far_lanes, -far, jnp.where(off_lanes, key_off, 0.0)).astype(BF16)
        logits = []
        for j in range(N_KV_B):
            k_ext = jnp.concatenate([k_ref[pl.ds(st, blk_sz), kv_cols[j]], kfeat], axis=1)
            for g in range(MOBA_GROUP):
                logits.append(lax.dot_general(k_ext, qext_scr[j * MOBA_GROUP + g], NT_DIMS,
                                              preferred_element_type=F32))
        for h in range(N_HEADS_B):
            t = logits[h] * to_log2
            if own:
                t = t + causal
                m_new = jnp.max(t, axis=0, keepdims=True)
                p = jnp.exp2(t - m_new)
            else:
                bias = bias_scr[h, pl.ds(i, 1), :]
                m_old = m_scr[h][0:1, :]
                m_new = jnp.maximum(m_old, jnp.max(logits[h], axis=0, keepdims=True) * to_log2 + bias)
                p = jnp.exp2(t + (bias - m_new))
            l_new = jnp.sum(p, axis=0, keepdims=True)
            acc = jnp.dot(vt_ref[i, kv_cols[h // MOBA_GROUP], :], p.astype(BF16), preferred_element_type=F32)
            if not own:
                alpha = jnp.exp2(m_old - m_new)
                l_new = alpha * l_scr[h][0:1, :] + l_new
                acc = alpha * acc_scr[h] + acc
            m_scr[h] = jnp.broadcast_to(m_new, (8, blk_sz))
            l_scr[h] = jnp.broadcast_to(l_new, (8, blk_sz))
            acc_scr[h] = acc

    attend(n, own=True)

    def past_block(i, carry):
        attend(i, own=False)
        return carry

    lax.fori_loop(0, n, past_block, 0)

    for h in range(N_HEADS_B):
        y_scr[:, head_cols[h]] = (acc_scr[h] / l_scr[h][0:1, :]).T
    o_ref[...] = _rms(y_scr[...], g_ref[...]).astype(o_ref.dtype)


def _moba(proj, slopes, norm_b):
    bsz, s, _ = proj.shape
    nblk = s // MOBA_BLOCK
    v_t = proj[:, :, OFF_VB:OFF_VB + KV_B].reshape(bsz, nblk, MOBA_BLOCK, KV_B).transpose(0, 1, 3, 2)
    q_width = MOBA_GROUP * HEAD_DIM

    def q_block(j, b, n):
        return (b, n, OFF_QB // q_width + j)

    return pl.pallas_call(
        _moba_kernel,
        grid=(bsz, nblk),
        in_specs=[pl.BlockSpec(memory_space=pltpu.SMEM),
                  *[pl.BlockSpec((None, MOBA_BLOCK, q_width), functools.partial(q_block, j))
                    for j in range(N_KV_B)],
                  pl.BlockSpec((None, s, KV_B), lambda b, n: (b, 0, OFF_KB // KV_B)),
                  pl.BlockSpec((None, nblk, KV_B, MOBA_BLOCK), lambda b, n: (b, 0, 0, 0)),
                  pl.BlockSpec((1, WIDTH_B), lambda b, n: (0, 0))],
        out_specs=pl.BlockSpec((None, MOBA_BLOCK, WIDTH_B), lambda b, n: (b, n, 0)),
        out_shape=jax.ShapeDtypeStruct((bsz, s, WIDTH_B), BF16),
        scratch_shapes=[pltpu.VMEM((N_KV_B, 2 * nblk, HEAD_DIM), BF16),
                        pltpu.VMEM((N_HEADS_B, MOBA_BLOCK, 2 * HEAD_DIM), BF16),
                        pltpu.VMEM((N_HEADS_B, nblk, MOBA_BLOCK), F32),
                        pltpu.VMEM((N_HEADS_B, 8, MOBA_BLOCK), F32),
                        pltpu.VMEM((N_HEADS_B, 8, MOBA_BLOCK), F32),
                        pltpu.VMEM((N_HEADS_B, HEAD_DIM, MOBA_BLOCK), F32),
                        pltpu.VMEM((MOBA_BLOCK, WIDTH_B), F32)],
        compiler_params=_params(2, 40),
        name="moba",
    )(slopes, *([proj] * N_KV_B), proj, v_t, norm_b.reshape(1, WIDTH_B))


def _alibi_slopes(n):
    return 2.0 ** (-8.0 * jnp.arange(1, n + 1, dtype=F32) / n)


def _ffn(h, w_gate, w_up, w_down):
    act = _gate_up(h, w_gate, w_up)
    return _matmul([act], w_down.astype(BF16), tm=512, tn=512, vmem_mib=58, name="ffn_down")


def kernel(x, c, w_ada, b_ada, ffn1_norm_pre, ffn1_norm_post, ffn1_w_gate, ffn1_w_up, ffn1_w_down, mix_norm_pre, mix_norm_post, w_in, attn_sinks, norm_a, norm_b, w_out, ffn2_norm_pre, ffn2_norm_post, ffn2_w_gate, ffn2_w_up, ffn2_w_down):
    bsz, s, d = x.shape
    depth = w_ada.shape[0]
    slopes_a = _alibi_slopes(N_HEADS_A)
    slopes_b = _alibi_slopes(N_HEADS_B)
    for l in range(depth):
        mod = _ada(c, w_ada[l], b_ada[l]).reshape(bsz, N_MOD, 1, d)
        shift, scale, gate = (lambda i: mod[:, 3 * i]), (lambda i: mod[:, 3 * i + 1]), (lambda i: mod[:, 3 * i + 2])

        h = _norm_mod(x, pre=(ffn1_norm_pre[l], scale(0), shift(0)))[0]
        y = _ffn(h.reshape(bsz * s, d), ffn1_w_gate[l], ffn1_w_up[l], ffn1_w_down[l])

        x, h = _norm_mod(x, post=(y.reshape(bsz, s, d), ffn1_norm_post[l], gate(0)),
                         pre=(mix_norm_pre[l], scale(1), shift(1)), res_w=0.5)
        proj = _matmul([h.reshape(bsz * s, d)], w_in[l].astype(BF16), tm=2048, tn=512, vmem_mib=48, name="w_in")
        proj = proj.reshape(bsz, s, D_IN)
        ya = _swa(proj, attn_sinks[l], slopes_a, norm_a[l])
        yb = _moba(proj, slopes_b, norm_b[l])
        y = _matmul([ya.reshape(bsz * s, WIDTH_A), yb.reshape(bsz * s, WIDTH_B)], w_out[l].astype(BF16),
                    tm=2048, tn=512, vmem_mib=48, name="w_out")

        x, h = _norm_mod(x, post=(y.reshape(bsz, s, d), mix_norm_post[l], gate(1)),
                         pre=(ffn2_norm_pre[l], scale(2), shift(2)), res_w=1.0)
        y = _ffn(h.reshape(bsz * s, d), ffn2_w_gate[l], ffn2_w_up[l], ffn2_w_down[l])
        x = _norm_mod(x, post=(y.reshape(bsz, s, d), ffn2_norm_post[l], gate(2)), res_w=0.5)[0]
    return x
```

```python
import functools
import math

import jax
import jax.numpy as jnp
from jax import lax
from jax.experimental import pallas as pl
from jax.experimental.pallas import tpu as pltpu

HEAD_DIM = 128
N_HEADS_A = 16
N_KV_A = 2
WINDOW = 128
N_HEADS_B = 16
N_KV_B = 4
MOBA_BLOCK = 256
MOBA_TOPK = 3
EPS = 1e-6
NEG_INF = -1e30
N_MOD = 9

WIDTH_A = N_HEADS_A * HEAD_DIM
WIDTH_B = N_HEADS_B * HEAD_DIM
KV_A = N_KV_A * HEAD_DIM
KV_B = N_KV_B * HEAD_DIM
D_IN = WIDTH_A + 2 * KV_A + WIDTH_B + 2 * KV_B

OFF_QA = 0
OFF_KA = OFF_QA + WIDTH_A
OFF_VA = OFF_KA + KV_A
OFF_QB = OFF_VA + KV_A
OFF_KB = OFF_QB + WIDTH_B
OFF_VB = OFF_KB + KV_B

F32 = jnp.float32
BF16 = jnp.bfloat16
NT_DIMS = (((1,), (1,)), ((), ()))

V7X_VMEM_BYTES = 64 * 1024 * 1024

GATE_UP_TILE = (2048, 256)
FFN_DOWN_TILE = (512, 512)
PROJ_TILE = (2048, 512)
ADA_COLS = 1024
NORM_ROWS = 256


def _nbytes(shape, dtype):
    return math.prod(shape) * jnp.dtype(dtype).itemsize


def _params(n_grid, windows, held=()):
    need = 2 * sum(_nbytes(*w) for w in windows) + sum(_nbytes(*h) for h in held)
    assert need <= V7X_VMEM_BYTES, (need, V7X_VMEM_BYTES)
    return pltpu.CompilerParams(dimension_semantics=("arbitrary",) * n_grid, vmem_limit_bytes=need)


def _rms(v, g):
    return v * lax.rsqrt(jnp.mean(v * v, axis=-1, keepdims=True) + EPS) * g


def _ada_kernel(c_ref, w_ref, b_ref, o_ref):
    c = c_ref[...]
    s = c * jax.nn.sigmoid(c)
    hi = s.astype(BF16)
    lo = (s - hi.astype(F32)).astype(BF16)
    lhs = jnp.concatenate([hi, lo], axis=0)
    r = jnp.dot(lhs, w_ref[...].astype(BF16), preferred_element_type=F32)
    o_ref[...] = r[:8] + r[8:] + b_ref[...]


def _ada(c, w, b):
    bsz, d = c.shape
    n = w.shape[1]
    tn = ADA_COLS
    c8 = jnp.pad(c, ((0, 8 - bsz), (0, 0)))
    out = pl.pallas_call(
        _ada_kernel,
        grid=(n // tn,),
        in_specs=[pl.BlockSpec((8, d), lambda j: (0, 0)),
                  pl.BlockSpec((d, tn), lambda j: (0, j)),
                  pl.BlockSpec((1, tn), lambda j: (0, j))],
        out_specs=pl.BlockSpec((8, tn), lambda j: (0, j)),
        out_shape=jax.ShapeDtypeStruct((8, n), F32),
        compiler_params=_params(1, [((8, d), F32), ((d, tn), F32), ((1, tn), F32), ((8, tn), F32)],
                                held=[((d, tn), BF16)]),
        name="ada",
    )(c8, w, b.reshape(1, n))
    return out[:bsz]


def _norm_mod_kernel(*refs, has_post, has_pre, res_w):
    refs = list(refs)
    x = refs.pop(0)[...]
    if has_post:
        y_ref, gpost_ref, gate_ref = refs.pop(0), refs.pop(0), refs.pop(0)
    if has_pre:
        gpre_ref, scale_ref, shift_ref = refs.pop(0), refs.pop(0), refs.pop(0)
    if has_post:
        x = x + res_w * gate_ref[...] * _rms(y_ref[...].astype(F32), gpost_ref[...])
        refs.pop(0)[...] = x
    if has_pre:
        refs.pop(0)[...] = (_rms(x, gpre_ref[...]) * (1.0 + scale_ref[...]) + shift_ref[...]).astype(BF16)


def _norm_mod(x, post=None, pre=None, res_w=1.0):
    bsz, s, d = x.shape
    ts = NORM_ROWS
    row = pl.BlockSpec((None, ts, d), lambda b, t: (b, t, 0))
    gain = pl.BlockSpec((1, d), lambda b, t: (0, 0))
    mod = pl.BlockSpec((None, 1, d), lambda b, t: (b, 0, 0))
    args, in_specs, out_specs, out_shape = [x], [row], [], []
    if post is not None:
        y, g_post, gate = post
        args += [y, g_post.reshape(1, d), gate]
        in_specs += [row, gain, mod]
        out_specs.append(row)
        out_shape.append(jax.ShapeDtypeStruct((bsz, s, d), F32))
    if pre is not None:
        g_pre, scale, shift = pre
        args += [g_pre.reshape(1, d), scale, shift]
        in_specs += [gain, mod, mod]
        out_specs.append(row)
        out_shape.append(jax.ShapeDtypeStruct((bsz, s, d), BF16))
    return pl.pallas_call(
        functools.partial(_norm_mod_kernel, has_post=post is not None, has_pre=pre is not None, res_w=res_w),
        grid=(bsz, s // ts),
        in_specs=in_specs, out_specs=out_specs, out_shape=out_shape,
        compiler_params=_params(2, [((ts, d), F32)] * 2 + [((ts, d), BF16)] * 2 + [((1, d), F32)] * 5,
                                held=[((ts, d), F32)] * 4),
        name="norm_mod",
    )(*args)


def _gate_up_kernel(h_ref, wg_ref, wu_ref, o_ref):
    h = h_ref[...]
    g = jnp.dot(h, wg_ref[...].astype(BF16), preferred_element_type=F32)
    u = jnp.dot(h, wu_ref[...].astype(BF16), preferred_element_type=F32)
    o_ref[...] = (g * jax.nn.sigmoid(g) * u).astype(o_ref.dtype)


def _gate_up(h, wg, wu):
    m, k = h.shape
    f = wg.shape[1]
    tm, tf = GATE_UP_TILE
    return pl.pallas_call(
        _gate_up_kernel,
        grid=(m // tm, f // tf),
        in_specs=[pl.BlockSpec((tm, k), lambda i, j: (i, 0)),
                  pl.BlockSpec((k, tf), lambda i, j: (0, j)),
                  pl.BlockSpec((k, tf), lambda i, j: (0, j))],
        out_specs=pl.BlockSpec((tm, tf), lambda i, j: (i, j)),
        out_shape=jax.ShapeDtypeStruct((m, f), BF16),
        compiler_params=_params(2, [((tm, k), BF16), ((k, tf), F32), ((k, tf), F32), ((tm, tf), BF16)],
                                held=[((k, tf), BF16)] * 2 + [((tm, tf), F32)] * 2),
        name="gate_up",
    )(h, wg, wu)


def _mm_kernel(*refs, splits):
    w_ref, o_ref = refs[len(splits)], refs[-1]
    acc, off = None, 0
    for a_ref, k in zip(refs, splits):
        part = jnp.dot(a_ref[...], w_ref[off:off + k, :], preferred_element_type=F32)
        acc = part if acc is None else acc + part
        off += k
    o_ref[...] = acc.astype(o_ref.dtype)


def _matmul(a_parts, w, tile, name):
    tm, tn = tile
    m = a_parts[0].shape[0]
    n = w.shape[1]
    splits = tuple(a.shape[1] for a in a_parts)
    return pl.pallas_call(
        functools.partial(_mm_kernel, splits=splits),
        grid=(m // tm, n // tn),
        in_specs=[pl.BlockSpec((tm, k), lambda i, j: (i, 0)) for k in splits]
        + [pl.BlockSpec((sum(splits), tn), lambda i, j: (0, j))],
        out_specs=pl.BlockSpec((tm, tn), lambda i, j: (i, j)),
        out_shape=jax.ShapeDtypeStruct((m, n), BF16),
        compiler_params=_params(2, [((tm, k), BF16) for k in splits] + [((sum(splits), tn), BF16), ((tm, tn), BF16)],
                                held=[((tm, tn), F32)] * 2),
        name=name,
    )(*a_parts, w)


SWA_TQ = 512
SWA_GROUP = N_HEADS_A // N_KV_A


def _swa_kernel(sink_ref, slope_ref, q_ref, kown_ref, kprev_ref, vtown_ref, vtprev_ref, g_ref, o_ref,
                bias_scr, o_scr):
    b, t = pl.program_id(0), pl.program_id(1)
    to_log2 = (HEAD_DIM ** -0.5) * math.log2(math.e)
    keys, lanes = 2 * WINDOW, SWA_GROUP * WINDOW

    @pl.when((b == 0) & (t == 0))
    def _():
        key = lax.broadcasted_iota(jnp.int32, (keys, lanes), 0)
        qry = lax.broadcasted_iota(jnp.int32, (keys, lanes), 1) & (WINDOW - 1)
        dist = qry + WINDOW - key
        inside = (dist >= 0) & (dist < WINDOW)
        distf = dist.astype(F32) * math.log2(math.e)
        for j in range(N_KV_A):
            slope = jnp.concatenate(
                [jnp.full((keys, WINDOW), slope_ref[j * SWA_GROUP + g], F32) for g in range(SWA_GROUP)], axis=1)
            bias_scr[j] = jnp.where(inside, -(slope * distf), NEG_INF)

    first_key = jnp.where(t > 0, 0, WINDOW)
    no_prev = jnp.where(lax.broadcasted_iota(jnp.int32, (keys, lanes), 0) >= first_key, 0.0, NEG_INF)

    for j in range(N_KV_A):
        kvc = slice(j * HEAD_DIM, (j + 1) * HEAD_DIM)
        sink2 = jnp.concatenate(
            [jnp.full((1, WINDOW), sink_ref[j * SWA_GROUP + g], F32) for g in range(SWA_GROUP)], axis=1
        ) * math.log2(math.e)
        for s in range(SWA_TQ // WINDOW):
            rows = slice(s * WINDOW, (s + 1) * WINDOW)
            if s == 0:
                kk = jnp.concatenate([kprev_ref[:, kvc], kown_ref[0:WINDOW, kvc]], axis=0)
                vv_t = jnp.concatenate([vtprev_ref[kvc, :], vtown_ref[kvc, 0:WINDOW]], axis=1)
            else:
                kk = kown_ref[(s - 1) * WINDOW:(s + 1) * WINDOW, kvc]
                vv_t = vtown_ref[kvc, (s - 1) * WINDOW:(s + 1) * WINDOW]
            qs = jnp.concatenate(
                [q_ref[rows, (j * SWA_GROUP + g) * HEAD_DIM:(j * SWA_GROUP + g + 1) * HEAD_DIM]
                 for g in range(SWA_GROUP)], axis=0)
            sc = lax.dot_general(kk, qs, NT_DIMS, preferred_element_type=F32) * to_log2 + bias_scr[j]
            if s == 0:
                sc = sc + no_prev
            mx = jnp.maximum(jnp.max(sc, axis=0, keepdims=True), sink2)
            p = jnp.exp2(sc - mx)
            denom = jnp.sum(p, axis=0, keepdims=True) + jnp.exp2(sink2 - mx)
            o_t = jnp.dot(vv_t, p.astype(BF16), preferred_element_type=F32) / denom
            for g in range(SWA_GROUP):
                h = j * SWA_GROUP + g
                o_scr[rows, h * HEAD_DIM:(h + 1) * HEAD_DIM] = o_t[:, g * WINDOW:(g + 1) * WINDOW].T
    o_ref[...] = _rms(o_scr[...], g_ref[...]).astype(o_ref.dtype)


def _swa(proj, sinks, slopes, norm_a):
    bsz, s, _ = proj.shape
    wpt = SWA_TQ // WINDOW
    v_t = proj[:, :, OFF_VA:OFF_VA + KV_A].transpose(0, 2, 1)
    smem = pl.BlockSpec(memory_space=pltpu.SMEM)
    return pl.pallas_call(
        _swa_kernel,
        grid=(bsz, s // SWA_TQ),
        in_specs=[smem, smem,
                  pl.BlockSpec((None, SWA_TQ, WIDTH_A), lambda b, t: (b, t, OFF_QA // WIDTH_A)),
                  pl.BlockSpec((None, SWA_TQ, KV_A), lambda b, t: (b, t, OFF_KA // KV_A)),
                  pl.BlockSpec((None, WINDOW, KV_A), lambda b, t: (b, jnp.maximum(t * wpt - 1, 0), OFF_KA // KV_A)),
                  pl.BlockSpec((None, KV_A, SWA_TQ), lambda b, t: (b, 0, t)),
                  pl.BlockSpec((None, KV_A, WINDOW), lambda b, t: (b, 0, jnp.maximum(t * wpt - 1, 0))),
                  pl.BlockSpec((1, WIDTH_A), lambda b, t: (0, 0))],
        out_specs=pl.BlockSpec((None, SWA_TQ, WIDTH_A), lambda b, t: (b, t, 0)),
        out_shape=jax.ShapeDtypeStruct((bsz, s, WIDTH_A), BF16),
        scratch_shapes=[pltpu.VMEM((N_KV_A, 2 * WINDOW, SWA_GROUP * WINDOW), F32),
                        pltpu.VMEM((SWA_TQ, WIDTH_A), F32)],
        compiler_params=_params(
            2, [((SWA_TQ, WIDTH_A), BF16)] * 2 + [((SWA_TQ, KV_A), BF16)] * 2 + [((WINDOW, KV_A), BF16)] * 2
            + [((1, WIDTH_A), F32)],
            held=[((N_KV_A, 2 * WINDOW, SWA_GROUP * WINDOW), F32), ((SWA_TQ, WIDTH_A), F32)]
            + [((2 * WINDOW, SWA_GROUP * WINDOW), F32)] * 4 + [((SWA_TQ, WIDTH_A), F32)]),
        name="swa",
    )(sinks, slopes, proj, proj, proj, v_t, v_t, norm_a.reshape(1, WIDTH_A))


MOBA_GROUP = N_HEADS_B // N_KV_B
N_ALIBI_TERMS = 3


def _moba_kernel(slope_ref, q0_ref, q1_ref, q2_ref, q3_ref, k_ref, vt_ref, g_ref, o_ref,
                 kmean_scr, qext_scr, bias_scr, m_scr, l_scr, acc_scr, y_scr):
    q_refs = (q0_ref, q1_ref, q2_ref, q3_ref)
    group_cols = [slice(g * HEAD_DIM, (g + 1) * HEAD_DIM) for g in range(MOBA_GROUP)]
    n = pl.program_id(1)
    blk_sz = MOBA_BLOCK
    nblk = k_ref.shape[0] // blk_sz
    qrows = MOBA_GROUP * blk_sz
    scale = HEAD_DIM ** -0.5
    to_log2 = scale * math.log2(math.e)

    @pl.when(n == 0)
    def _():
        kmean = jnp.concatenate(
            [jnp.sum(k_ref[i * blk_sz:(i + 1) * blk_sz, :].astype(F32), axis=0, keepdims=True)
             for i in range(nblk)], axis=0) * (1.0 / blk_sz)
        hi = kmean.astype(BF16)
        lo = (kmean - hi.astype(F32)).astype(BF16)
        for j in range(N_KV_B):
            kc = slice(j * HEAD_DIM, (j + 1) * HEAD_DIM)
            kmean_scr[j] = jnp.concatenate([hi[:, kc], lo[:, kc]], axis=0)
        f_lane = lax.broadcasted_iota(jnp.int32, (blk_sz, HEAD_DIM), 1)
        for h in range(N_HEADS_B):
            a = jnp.full((blk_sz, HEAD_DIM), slope_ref[h], F32) * (1.0 / scale)
            a1 = a.astype(BF16).astype(F32)
            a2 = (a - a1).astype(BF16).astype(F32)
            a3 = (a - a1 - a2).astype(BF16).astype(F32)
            term = jnp.where((f_lane == 0) | (f_lane == 3), a1, jnp.where((f_lane == 1) | (f_lane == 4), a2, a3))
            qext_scr[h, :, HEAD_DIM:2 * HEAD_DIM] = jnp.where(f_lane < 2 * N_ALIBI_TERMS, term, 0.0).astype(BF16)

    lane = lax.broadcasted_iota(jnp.int32, (blk_sz, HEAD_DIM), 1)
    key_off = lax.broadcasted_iota(jnp.int32, (blk_sz, HEAD_DIM), 0).astype(F32)
    off_lanes = lane < N_ALIBI_TERMS
    far_lanes = (lane >= N_ALIBI_TERMS) & (lane < 2 * N_ALIBI_TERMS)
    own_feat = jnp.where(off_lanes, key_off, 0.0).astype(BF16)
    causal = jnp.where(lax.broadcasted_iota(jnp.int32, (blk_sz, blk_sz), 0)
                       <= lax.broadcasted_iota(jnp.int32, (blk_sz, blk_sz), 1), 0.0, NEG_INF)
    blk = lax.broadcasted_iota(jnp.int32, (nblk, qrows), 0)

    head_cols = [slice(h * HEAD_DIM, (h + 1) * HEAD_DIM) for h in range(N_HEADS_B)]
    kv_cols = [slice(j * HEAD_DIM, (j + 1) * HEAD_DIM) for j in range(N_KV_B)]

    for j in range(N_KV_B):
        qs = jnp.concatenate([q_refs[j][:, gc] for gc in group_cols], axis=0)
        g2 = lax.dot_general(kmean_scr[j], qs, NT_DIMS, preferred_element_type=F32)
        gate = g2[:nblk] + g2[nblk:]
        beaten = jnp.zeros((nblk, qrows), F32)
        for i in range(nblk):
            gi = gate[i:i + 1, :]
            beats = (gi > gate) | ((gi == gate) & (blk > i))
            beaten = beaten + jnp.where(beats, jnp.where(i < n, 1.0, 0.0), 0.0)
        bias_t = jnp.where((blk < n) & (beaten < MOBA_TOPK), 0.0, NEG_INF)

        for g in range(MOBA_GROUP):
            h = j * MOBA_GROUP + g
            bias_scr[h] = bias_t[:, g * blk_sz:(g + 1) * blk_sz]
            qext_scr[h, :, 0:HEAD_DIM] = q_refs[j][:, group_cols[g]]

    def attend(i, own):
        st = pl.multiple_of(i * blk_sz, blk_sz)
        if own:
            kfeat = own_feat
        else:
            far = ((n - i) * blk_sz).astype(F32)
            kfeat = jnp.where(far_lanes, -far, jnp.where(off_lanes, key_off, 0.0)).astype(BF16)
        logits = []
        for j in range(N_KV_B):
            k_ext = jnp.concatenate([k_ref[pl.ds(st, blk_sz), kv_cols[j]], kfeat], axis=1)
            for g in range(MOBA_GROUP):
                logits.append(lax.dot_general(k_ext, qext_scr[j * MOBA_GROUP + g], NT_DIMS,
                                              preferred_element_type=F32))
        for h in range(N_HEADS_B):
            t = logits[h] * to_log2
            if own:
                t = t + causal
                m_new = jnp.max(t, axis=0, keepdims=True)
                p = jnp.exp2(t - m_new)
            else:
                bias = bias_scr[h, pl.ds(i, 1), :]
                m_old = m_scr[h][0:1, :]
                m_new = jnp.maximum(m_old, jnp.max(logits[h], axis=0, keepdims=True) * to_log2 + bias)
                p = jnp.exp2(t + (bias - m_new))
            l_new = jnp.sum(p, axis=0, keepdims=True)
            acc = jnp.dot(vt_ref[i, kv_cols[h // MOBA_GROUP], :], p.astype(BF16), preferred_element_type=F32)
            if not own:
                alpha = jnp.exp2(m_old - m_new)
                l_new = alpha * l_scr[h][0:1, :] + l_new
                acc = alpha * acc_scr[h] + acc
            m_scr[h] = jnp.broadcast_to(m_new, (8, blk_sz))
            l_scr[h] = jnp.broadcast_to(l_new, (8, blk_sz))
            acc_scr[h] = acc

    attend(n, own=True)

    def past_block(i, carry):
        attend(i, own=False)
        return carry

    lax.fori_loop(0, n, past_block, 0)

    for h in range(N_HEADS_B):
        y_scr[:, head_cols[h]] = (acc_scr[h] / l_scr[h][0:1, :]).T
    o_ref[...] = _rms(y_scr[...], g_ref[...]).astype(o_ref.dtype)


def _moba(proj, slopes, norm_b):
    bsz, s, _ = proj.shape
    nblk = s // MOBA_BLOCK
    v_t = proj[:, :, OFF_VB:OFF_VB + KV_B].reshape(bsz, nblk, MOBA_BLOCK, KV_B).transpose(0, 1, 3, 2)
    q_width = MOBA_GROUP * HEAD_DIM

    def q_block(j, b, n):
        return (b, n, OFF_QB // q_width + j)

    return pl.pallas_call(
        _moba_kernel,
        grid=(bsz, nblk),
        in_specs=[pl.BlockSpec(memory_space=pltpu.SMEM),
                  *[pl.BlockSpec((None, MOBA_BLOCK, q_width), functools.partial(q_block, j))
                    for j in range(N_KV_B)],
                  pl.BlockSpec((None, s, KV_B), lambda b, n: (b, 0, OFF_KB // KV_B)),
                  pl.BlockSpec((None, nblk, KV_B, MOBA_BLOCK), lambda b, n: (b, 0, 0, 0)),
                  pl.BlockSpec((1, WIDTH_B), lambda b, n: (0, 0))],
        out_specs=pl.BlockSpec((None, MOBA_BLOCK, WIDTH_B), lambda b, n: (b, n, 0)),
        out_shape=jax.ShapeDtypeStruct((bsz, s, WIDTH_B), BF16),
        scratch_shapes=[pltpu.VMEM((N_KV_B, 2 * nblk, HEAD_DIM), BF16),
                        pltpu.VMEM((N_HEADS_B, MOBA_BLOCK, 2 * HEAD_DIM), BF16),
                        pltpu.VMEM((N_HEADS_B, nblk, MOBA_BLOCK), F32),
                        pltpu.VMEM((N_HEADS_B, 8, MOBA_BLOCK), F32),
                        pltpu.VMEM((N_HEADS_B, 8, MOBA_BLOCK), F32),
                        pltpu.VMEM((N_HEADS_B, HEAD_DIM, MOBA_BLOCK), F32),
                        pltpu.VMEM((MOBA_BLOCK, WIDTH_B), F32)],
        compiler_params=_params(
            2, [((MOBA_BLOCK, WIDTH_B), BF16)] * 2 + [((s, KV_B), BF16)] * 2 + [((1, WIDTH_B), F32)],
            held=[((N_HEADS_B, MOBA_BLOCK, 2 * HEAD_DIM), BF16), ((N_HEADS_B, nblk + 16, MOBA_BLOCK), F32),
                  ((N_HEADS_B, HEAD_DIM, MOBA_BLOCK), F32), ((MOBA_BLOCK, WIDTH_B), F32)]
            + [((N_HEADS_B, MOBA_BLOCK, MOBA_BLOCK), F32)] * 2 + [((MOBA_BLOCK, WIDTH_B), F32)]),
        name="moba",
    )(slopes, *([proj] * N_KV_B), proj, v_t, norm_b.reshape(1, WIDTH_B))


def _alibi_slopes(n):
    return 2.0 ** (-8.0 * jnp.arange(1, n + 1, dtype=F32) / n)


def _ffn(h, w_gate, w_up, w_down):
    act = _gate_up(h, w_gate, w_up)
    return _matmul([act], w_down.astype(BF16), FFN_DOWN_TILE, name="ffn_down")


def kernel(x, c, w_ada, b_ada, ffn1_norm_pre, ffn1_norm_post, ffn1_w_gate, ffn1_w_up, ffn1_w_down, mix_norm_pre, mix_norm_post, w_in, attn_sinks, norm_a, norm_b, w_out, ffn2_norm_pre, ffn2_norm_post, ffn2_w_gate, ffn2_w_up, ffn2_w_down):
    bsz, s, d = x.shape
    depth = w_ada.shape[0]
    slopes_a = _alibi_slopes(N_HEADS_A)
    slopes_b = _alibi_slopes(N_HEADS_B)
    for l in range(depth):
        mod = _ada(c, w_ada[l], b_ada[l]).reshape(bsz, N_MOD, 1, d)
        shift, scale, gate = (lambda i: mod[:, 3 * i]), (lambda i: mod[:, 3 * i + 1]), (lambda i: mod[:, 3 * i + 2])

        h = _norm_mod(x, pre=(ffn1_norm_pre[l], scale(0), shift(0)))[0]
        y = _ffn(h.reshape(bsz * s, d), ffn1_w_gate[l], ffn1_w_up[l], ffn1_w_down[l])

        x, h = _norm_mod(x, post=(y.reshape(bsz, s, d), ffn1_norm_post[l], gate(0)),
                         pre=(mix_norm_pre[l], scale(1), shift(1)), res_w=0.5)
        proj = _matmul([h.reshape(bsz * s, d)], w_in[l].astype(BF16), PROJ_TILE, name="w_in")
        proj = proj.reshape(bsz, s, D_IN)
        ya = _swa(proj, attn_sinks[l], slopes_a, norm_a[l])
        yb = _moba(proj, slopes_b, norm_b[l])
        y = _matmul([ya.reshape(bsz * s, WIDTH_A), yb.reshape(bsz * s, WIDTH_B)], w_out[l].astype(BF16),
                    PROJ_TILE, name="w_out")

        x, h = _norm_mod(x, post=(y.reshape(bsz, s, d), mix_norm_post[l], gate(1)),
                         pre=(ffn2_norm_pre[l], scale(2), shift(2)), res_w=1.0)
        y = _ffn(h.reshape(bsz * s, d), ffn2_w_gate[l], ffn2_w_up[l], ffn2_w_down[l])
        x = _norm_mod(x, post=(y.reshape(bsz, s, d), ffn2_norm_post[l], gate(2)), res_w=0.5)[0]
    return x
```

```python
import functools
import math

import jax
import jax.numpy as jnp
from jax import lax
from jax.experimental import pallas as pl
from jax.experimental.pallas import tpu as pltpu

HEAD_DIM = 128
N_HEADS_A = 16
N_KV_A = 2
WINDOW = 128
N_HEADS_B = 16
N_KV_B = 4
MOBA_BLOCK = 256
MOBA_TOPK = 3
EPS = 1e-6
NEG_INF = -1e30
N_MOD = 9

WIDTH_A = N_HEADS_A * HEAD_DIM
WIDTH_B = N_HEADS_B * HEAD_DIM
KV_A = N_KV_A * HEAD_DIM
KV_B = N_KV_B * HEAD_DIM
D_IN = WIDTH_A + 2 * KV_A + WIDTH_B + 2 * KV_B

OFF_QA = 0
OFF_KA = OFF_QA + WIDTH_A
OFF_VA = OFF_KA + KV_A
OFF_QB = OFF_VA + KV_A
OFF_KB = OFF_QB + WIDTH_B
OFF_VB = OFF_KB + KV_B

F32 = jnp.float32
BF16 = jnp.bfloat16
NT_DIMS = (((1,), (1,)), ((), ()))

V7X_VMEM_BYTES = 64 * 1024 * 1024

GATE_UP_TILE = (2048, 256)
FFN_DOWN_TILE = (512, 512)
PROJ_TILE = (2048, 512)
ADA_COLS = 1024
NORM_ROWS = 256


def _nbytes(shape, dtype):
    return math.prod(shape) * jnp.dtype(dtype).itemsize


def _params(n_grid, windows, held=()):
    need = 2 * sum(_nbytes(*w) for w in windows) + sum(_nbytes(*h) for h in held)
    assert need <= V7X_VMEM_BYTES, (need, V7X_VMEM_BYTES)
    return pltpu.CompilerParams(dimension_semantics=("arbitrary",) * n_grid, vmem_limit_bytes=need)


def _rms(v, g):
    return v * lax.rsqrt(jnp.mean(v * v, axis=-1, keepdims=True) + EPS) * g


def _ada_kernel(c_ref, w_ref, b_ref, o_ref):
    c = c_ref[...]
    s = c * jax.nn.sigmoid(c)
    hi = s.astype(BF16)
    lo = (s - hi.astype(F32)).astype(BF16)
    lhs = jnp.concatenate([hi, lo], axis=0)
    r = jnp.dot(lhs, w_ref[...].astype(BF16), preferred_element_type=F32)
    o_ref[...] = r[:8] + r[8:] + b_ref[...]


def _ada(c, w, b):
    bsz, d = c.shape
    n = w.shape[1]
    tn = ADA_COLS
    c8 = jnp.pad(c, ((0, 8 - bsz), (0, 0)))
    out = pl.pallas_call(
        _ada_kernel,
        grid=(n // tn,),
        in_specs=[pl.BlockSpec((8, d), lambda j: (0, 0)),
                  pl.BlockSpec((d, tn), lambda j: (0, j)),
                  pl.BlockSpec((1, tn), lambda j: (0, j))],
        out_specs=pl.BlockSpec((8, tn), lambda j: (0, j)),
        out_shape=jax.ShapeDtypeStruct((8, n), F32),
        compiler_params=_params(1, [((8, d), F32), ((d, tn), F32), ((1, tn), F32), ((8, tn), F32)],
                                held=[((d, tn), BF16)]),
        name="ada",
    )(c8, w, b.reshape(1, n))
    return out[:bsz]


def _norm_mod_kernel(*refs, has_post, has_pre, res_w):
    refs = list(refs)
    x = refs.pop(0)[...]
    if has_post:
        y_ref, gpost_ref, gate_ref = refs.pop(0), refs.pop(0), refs.pop(0)
    if has_pre:
        gpre_ref, scale_ref, shift_ref = refs.pop(0), refs.pop(0), refs.pop(0)
    if has_post:
        x = x + res_w * gate_ref[...] * _rms(y_ref[...].astype(F32), gpost_ref[...])
        refs.pop(0)[...] = x
    if has_pre:
        refs.pop(0)[...] = (_rms(x, gpre_ref[...]) * (1.0 + scale_ref[...]) + shift_ref[...]).astype(BF16)


def _norm_mod(x, post=None, pre=None, res_w=1.0):
    bsz, s, d = x.shape
    ts = NORM_ROWS
    row = pl.BlockSpec((None, ts, d), lambda b, t: (b, t, 0))
    gain = pl.BlockSpec((1, d), lambda b, t: (0, 0))
    mod = pl.BlockSpec((None, 1, d), lambda b, t: (b, 0, 0))
    args, in_specs, out_specs, out_shape = [x], [row], [], []
    if post is not None:
        y, g_post, gate = post
        args += [y, g_post.reshape(1, d), gate]
        in_specs += [row, gain, mod]
        out_specs.append(row)
        out_shape.append(jax.ShapeDtypeStruct((bsz, s, d), F32))
    if pre is not None:
        g_pre, scale, shift = pre
        args += [g_pre.reshape(1, d), scale, shift]
        in_specs += [gain, mod, mod]
        out_specs.append(row)
        out_shape.append(jax.ShapeDtypeStruct((bsz, s, d), BF16))
    return pl.pallas_call(
        functools.partial(_norm_mod_kernel, has_post=post is not None, has_pre=pre is not None, res_w=res_w),
        grid=(bsz, s // ts),
        in_specs=in_specs, out_specs=out_specs, out_shape=out_shape,
        compiler_params=_params(2, [((ts, d), F32)] * 2 + [((ts, d), BF16)] * 2 + [((1, d), F32)] * 5,
                                held=[((ts, d), F32)] * 4),
        name="norm_mod",
    )(*args)


def _gate_up_kernel(h_ref, wg_ref, wu_ref, o_ref):
    h = h_ref[...]
    g = jnp.dot(h, wg_ref[...].astype(BF16), preferred_element_type=F32)
    u = jnp.dot(h, wu_ref[...].astype(BF16), preferred_element_type=F32)
    o_ref[...] = (g * jax.nn.sigmoid(g) * u).astype(o_ref.dtype)


def _gate_up(h, wg, wu):
    m, k = h.shape
    f = wg.shape[1]
    tm, tf = GATE_UP_TILE
    return pl.pallas_call(
        _gate_up_kernel,
        grid=(m // tm, f // tf),
        in_specs=[pl.BlockSpec((tm, k), lambda i, j: (i, 0)),
                  pl.BlockSpec((k, tf), lambda i, j: (0, j)),
                  pl.BlockSpec((k, tf), lambda i, j: (0, j))],
        out_specs=pl.BlockSpec((tm, tf), lambda i, j: (i, j)),
        out_shape=jax.ShapeDtypeStruct((m, f), BF16),
        compiler_params=_params(2, [((tm, k), BF16), ((k, tf), F32), ((k, tf), F32), ((tm, tf), BF16)],
                                held=[((k, tf), BF16)] * 2 + [((tm, tf), F32)] * 2),
        name="gate_up",
    )(h, wg, wu)


def _mm_kernel(*refs, splits):
    w_ref, o_ref = refs[len(splits)], refs[-1]
    acc, off = None, 0
    for a_ref, k in zip(refs, splits):
        part = jnp.dot(a_ref[...], w_ref[off:off + k, :], preferred_element_type=F32)
        acc = part if acc is None else acc + part
        off += k
    o_ref[...] = acc.astype(o_ref.dtype)


def _matmul(a_parts, w, tile, name):
    tm, tn = tile
    m = a_parts[0].shape[0]
    n = w.shape[1]
    splits = tuple(a.shape[1] for a in a_parts)
    return pl.pallas_call(
        functools.partial(_mm_kernel, splits=splits),
        grid=(m // tm, n // tn),
        in_specs=[pl.BlockSpec((tm, k), lambda i, j: (i, 0)) for k in splits]
        + [pl.BlockSpec((sum(splits), tn), lambda i, j: (0, j))],
        out_specs=pl.BlockSpec((tm, tn), lambda i, j: (i, j)),
        out_shape=jax.ShapeDtypeStruct((m, n), BF16),
        compiler_params=_params(2, [((tm, k), BF16) for k in splits] + [((sum(splits), tn), BF16), ((tm, tn), BF16)],
                                held=[((tm, tn), F32)] * 2),
        name=name,
    )(*a_parts, w)


SWA_TQ = 1024
SWA_GROUP = N_HEADS_A // N_KV_A


def _swa_kernel(sink_ref, slope_ref, q_ref, kown_ref, kprev_ref, vtown_ref, vtprev_ref, g_ref, o_ref,
                bias_scr, o_scr):
    b, t = pl.program_id(0), pl.program_id(1)
    to_log2 = (HEAD_DIM ** -0.5) * math.log2(math.e)
    keys, lanes = 2 * WINDOW, SWA_GROUP * WINDOW

    @pl.when((b == 0) & (t == 0))
    def _():
        key = lax.broadcasted_iota(jnp.int32, (keys, lanes), 0)
        qry = lax.broadcasted_iota(jnp.int32, (keys, lanes), 1) & (WINDOW - 1)
        dist = qry + WINDOW - key
        inside = (dist >= 0) & (dist < WINDOW)
        distf = dist.astype(F32) * math.log2(math.e)
        for j in range(N_KV_A):
            slope = jnp.concatenate(
                [jnp.full((keys, WINDOW), slope_ref[j * SWA_GROUP + g], F32) for g in range(SWA_GROUP)], axis=1)
            bias_scr[j] = jnp.where(inside, -(slope * distf), NEG_INF)

    first_key = jnp.where(t > 0, 0, WINDOW)
    no_prev = jnp.where(lax.broadcasted_iota(jnp.int32, (keys, lanes), 0) >= first_key, 0.0, NEG_INF)

    for j in range(N_KV_A):
        kvc = slice(j * HEAD_DIM, (j + 1) * HEAD_DIM)
        sink2 = jnp.concatenate(
            [jnp.full((1, WINDOW), sink_ref[j * SWA_GROUP + g], F32) for g in range(SWA_GROUP)], axis=1
        ) * math.log2(math.e)
        for s in range(SWA_TQ // WINDOW):
            rows = slice(s * WINDOW, (s + 1) * WINDOW)
            if s == 0:
                kk = jnp.concatenate([kprev_ref[:, kvc], kown_ref[0:WINDOW, kvc]], axis=0)
                vv_t = jnp.concatenate([vtprev_ref[kvc, :], vtown_ref[kvc, 0:WINDOW]], axis=1)
            else:
                kk = kown_ref[(s - 1) * WINDOW:(s + 1) * WINDOW, kvc]
                vv_t = vtown_ref[kvc, (s - 1) * WINDOW:(s + 1) * WINDOW]
            qs = jnp.concatenate(
                [q_ref[rows, (j * SWA_GROUP + g) * HEAD_DIM:(j * SWA_GROUP + g + 1) * HEAD_DIM]
                 for g in range(SWA_GROUP)], axis=0)
            sc = lax.dot_general(kk, qs, NT_DIMS, preferred_element_type=F32) * to_log2 + bias_scr[j]
            if s == 0:
                sc = sc + no_prev
            mx = jnp.maximum(jnp.max(sc, axis=0, keepdims=True), sink2)
            p = jnp.exp2(sc - mx)
            denom = jnp.sum(p, axis=0, keepdims=True) + jnp.exp2(sink2 - mx)
            o_t = jnp.dot(vv_t, p.astype(BF16), preferred_element_type=F32) / denom
            for g in range(SWA_GROUP):
                h = j * SWA_GROUP + g
                o_scr[rows, h * HEAD_DIM:(h + 1) * HEAD_DIM] = o_t[:, g * WINDOW:(g + 1) * WINDOW].T
    o_ref[...] = _rms(o_scr[...], g_ref[...]).astype(o_ref.dtype)


def _swa(proj, sinks, slopes, norm_a):
    bsz, s, _ = proj.shape
    wpt = SWA_TQ // WINDOW
    v_t = proj[:, :, OFF_VA:OFF_VA + KV_A].transpose(0, 2, 1)
    smem = pl.BlockSpec(memory_space=pltpu.SMEM)
    return pl.pallas_call(
        _swa_kernel,
        grid=(bsz, s // SWA_TQ),
        in_specs=[smem, smem,
                  pl.BlockSpec((None, SWA_TQ, WIDTH_A), lambda b, t: (b, t, OFF_QA // WIDTH_A)),
                  pl.BlockSpec((None, SWA_TQ, KV_A), lambda b, t: (b, t, OFF_KA // KV_A)),
                  pl.BlockSpec((None, WINDOW, KV_A), lambda b, t: (b, jnp.maximum(t * wpt - 1, 0), OFF_KA // KV_A)),
                  pl.BlockSpec((None, KV_A, SWA_TQ), lambda b, t: (b, 0, t)),
                  pl.BlockSpec((None, KV_A, WINDOW), lambda b, t: (b, 0, jnp.maximum(t * wpt - 1, 0))),
                  pl.BlockSpec((1, WIDTH_A), lambda b, t: (0, 0))],
        out_specs=pl.BlockSpec((None, SWA_TQ, WIDTH_A), lambda b, t: (b, t, 0)),
        out_shape=jax.ShapeDtypeStruct((bsz, s, WIDTH_A), BF16),
        scratch_shapes=[pltpu.VMEM((N_KV_A, 2 * WINDOW, SWA_GROUP * WINDOW), F32),
                        pltpu.VMEM((SWA_TQ, WIDTH_A), F32)],
        compiler_params=_params(
            2, [((SWA_TQ, WIDTH_A), BF16)] * 2 + [((SWA_TQ, KV_A), BF16)] * 2 + [((WINDOW, KV_A), BF16)] * 2
            + [((1, WIDTH_A), F32)],
            held=[((N_KV_A, 2 * WINDOW, SWA_GROUP * WINDOW), F32), ((SWA_TQ, WIDTH_A), F32)]
            + [((2 * WINDOW, SWA_GROUP * WINDOW), F32)] * 4 + [((SWA_TQ, WIDTH_A), F32)]),
        name="swa",
    )(sinks, slopes, proj, proj, proj, v_t, v_t, norm_a.reshape(1, WIDTH_A))


MOBA_GROUP = N_HEADS_B // N_KV_B
N_ALIBI_TERMS = 3


def _moba_kernel(slope_ref, qt_ref, k_ref, vt_ref, g_ref, o_ref,
                 kmean_scr, qext_scr, bias_scr, m_scr, l_scr, acc_scr, y_scr):
    n = pl.program_id(1)
    blk_sz = MOBA_BLOCK
    nblk = k_ref.shape[0] // blk_sz
    qrows = MOBA_GROUP * blk_sz
    scale = HEAD_DIM ** -0.5
    to_log2 = scale * math.log2(math.e)

    @pl.when(n == 0)
    def _():
        kmean = jnp.concatenate(
            [jnp.sum(k_ref[i * blk_sz:(i + 1) * blk_sz, :].astype(F32), axis=0, keepdims=True)
             for i in range(nblk)], axis=0) * (1.0 / blk_sz)
        hi = kmean.astype(BF16)
        lo = (kmean - hi.astype(F32)).astype(BF16)
        for j in range(N_KV_B):
            kc = slice(j * HEAD_DIM, (j + 1) * HEAD_DIM)
            kmean_scr[j] = jnp.concatenate([hi[:, kc], lo[:, kc]], axis=0)
        f_row = lax.broadcasted_iota(jnp.int32, (HEAD_DIM, blk_sz), 0)
        for h in range(N_HEADS_B):
            a = jnp.full((HEAD_DIM, blk_sz), slope_ref[h], F32) * (1.0 / scale)
            a1 = a.astype(BF16).astype(F32)
            a2 = (a - a1).astype(BF16).astype(F32)
            a3 = (a - a1 - a2).astype(BF16).astype(F32)
            term = jnp.where((f_row == 0) | (f_row == 3), a1, jnp.where((f_row == 1) | (f_row == 4), a2, a3))
            qext_scr[h, HEAD_DIM:2 * HEAD_DIM, :] = jnp.where(f_row < 2 * N_ALIBI_TERMS, term, 0.0).astype(BF16)

    lane = lax.broadcasted_iota(jnp.int32, (blk_sz, HEAD_DIM), 1)
    key_off = lax.broadcasted_iota(jnp.int32, (blk_sz, HEAD_DIM), 0).astype(F32)
    off_lanes = lane < N_ALIBI_TERMS
    far_lanes = (lane >= N_ALIBI_TERMS) & (lane < 2 * N_ALIBI_TERMS)
    own_feat = jnp.where(off_lanes, key_off, 0.0).astype(BF16)
    causal = jnp.where(lax.broadcasted_iota(jnp.int32, (blk_sz, blk_sz), 0)
                       <= lax.broadcasted_iota(jnp.int32, (blk_sz, blk_sz), 1), 0.0, NEG_INF)
    blk = lax.broadcasted_iota(jnp.int32, (nblk, qrows), 0)

    head_cols = [slice(h * HEAD_DIM, (h + 1) * HEAD_DIM) for h in range(N_HEADS_B)]
    kv_cols = [slice(j * HEAD_DIM, (j + 1) * HEAD_DIM) for j in range(N_KV_B)]

    for j in range(N_KV_B):
        qs_t = jnp.concatenate([qt_ref[head_cols[j * MOBA_GROUP + g], :] for g in range(MOBA_GROUP)], axis=1)
        g2 = jnp.dot(kmean_scr[j], qs_t, preferred_element_type=F32)
        gate = g2[:nblk] + g2[nblk:]
        beaten = jnp.zeros((nblk, qrows), F32)
        for i in range(nblk):
            gi = gate[i:i + 1, :]
            beats = (gi > gate) | ((gi == gate) & (blk > i))
            beaten = beaten + jnp.where(beats, jnp.where(i < n, 1.0, 0.0), 0.0)
        bias_t = jnp.where((blk < n) & (beaten < MOBA_TOPK), 0.0, NEG_INF)

        for g in range(MOBA_GROUP):
            h = j * MOBA_GROUP + g
            bias_scr[h] = bias_t[:, g * blk_sz:(g + 1) * blk_sz]
            qext_scr[h, 0:HEAD_DIM, :] = qt_ref[head_cols[h], :]

    def attend(i, own):
        st = pl.multiple_of(i * blk_sz, blk_sz)
        if own:
            kfeat = own_feat
        else:
            far = ((n - i) * blk_sz).astype(F32)
            kfeat = jnp.where(far_lanes, -far, jnp.where(off_lanes, key_off, 0.0)).astype(BF16)
        logits = []
        for j in range(N_KV_B):
            k_ext = jnp.concatenate([k_ref[pl.ds(st, blk_sz), kv_cols[j]], kfeat], axis=1)
            for g in range(MOBA_GROUP):
                logits.append(jnp.dot(k_ext, qext_scr[j * MOBA_GROUP + g], preferred_element_type=F32))
        for h in range(N_HEADS_B):
            t = logits[h] * to_log2
            if own:
                t = t + causal
                m_new = jnp.max(t, axis=0, keepdims=True)
                p = jnp.exp2(t - m_new)
            else:
                bias = bias_scr[h, pl.ds(i, 1), :]
                m_old = m_scr[h][0:1, :]
                m_new = jnp.maximum(m_old, jnp.max(logits[h], axis=0, keepdims=True) * to_log2 + bias)
                p = jnp.exp2(t + (bias - m_new))
            l_new = jnp.sum(p, axis=0, keepdims=True)
            acc = jnp.dot(vt_ref[i, kv_cols[h // MOBA_GROUP], :], p.astype(BF16), preferred_element_type=F32)
            if not own:
                alpha = jnp.exp2(m_old - m_new)
                l_new = alpha * l_scr[h][0:1, :] + l_new
                acc = alpha * acc_scr[h] + acc
            m_scr[h] = jnp.broadcast_to(m_new, (8, blk_sz))
            l_scr[h] = jnp.broadcast_to(l_new, (8, blk_sz))
            acc_scr[h] = acc

    attend(n, own=True)

    def past_block(i, carry):
        attend(i, own=False)
        return carry

    lax.fori_loop(0, n, past_block, 0)

    for h in range(N_HEADS_B):
        y_scr[:, head_cols[h]] = (acc_scr[h] / l_scr[h][0:1, :]).T
    o_ref[...] = _rms(y_scr[...], g_ref[...]).astype(o_ref.dtype)


def _moba(proj, slopes, norm_b):
    bsz, s, _ = proj.shape
    nblk = s // MOBA_BLOCK
    v_t = proj[:, :, OFF_VB:OFF_VB + KV_B].reshape(bsz, nblk, MOBA_BLOCK, KV_B).transpose(0, 1, 3, 2)
    q_t = proj[:, :, OFF_QB:OFF_QB + WIDTH_B].transpose(0, 2, 1)
    return pl.pallas_call(
        _moba_kernel,
        grid=(bsz, nblk),
        in_specs=[pl.BlockSpec(memory_space=pltpu.SMEM),
                  pl.BlockSpec((None, WIDTH_B, MOBA_BLOCK), lambda b, n: (b, 0, n)),
                  pl.BlockSpec((None, s, KV_B), lambda b, n: (b, 0, OFF_KB // KV_B)),
                  pl.BlockSpec((None, nblk, KV_B, MOBA_BLOCK), lambda b, n: (b, 0, 0, 0)),
                  pl.BlockSpec((1, WIDTH_B), lambda b, n: (0, 0))],
        out_specs=pl.BlockSpec((None, MOBA_BLOCK, WIDTH_B), lambda b, n: (b, n, 0)),
        out_shape=jax.ShapeDtypeStruct((bsz, s, WIDTH_B), BF16),
        scratch_shapes=[pltpu.VMEM((N_KV_B, 2 * nblk, HEAD_DIM), BF16),
                        pltpu.VMEM((N_HEADS_B, 2 * HEAD_DIM, MOBA_BLOCK), BF16),
                        pltpu.VMEM((N_HEADS_B, nblk, MOBA_BLOCK), F32),
                        pltpu.VMEM((N_HEADS_B, 8, MOBA_BLOCK), F32),
                        pltpu.VMEM((N_HEADS_B, 8, MOBA_BLOCK), F32),
                        pltpu.VMEM((N_HEADS_B, HEAD_DIM, MOBA_BLOCK), F32),
                        pltpu.VMEM((MOBA_BLOCK, WIDTH_B), F32)],
        compiler_params=_params(
            2, [((MOBA_BLOCK, WIDTH_B), BF16)] * 2 + [((s, KV_B), BF16)] * 2 + [((1, WIDTH_B), F32)],
            held=[((N_HEADS_B, MOBA_BLOCK, 2 * HEAD_DIM), BF16), ((N_HEADS_B, nblk + 16, MOBA_BLOCK), F32),
                  ((N_HEADS_B, HEAD_DIM, MOBA_BLOCK), F32), ((MOBA_BLOCK, WIDTH_B), F32)]
            + [((N_HEADS_B, MOBA_BLOCK, MOBA_BLOCK), F32)] * 2 + [((MOBA_BLOCK, WIDTH_B), F32)]),
        name="moba",
    )(slopes, q_t, proj, v_t, norm_b.reshape(1, WIDTH_B))


def _alibi_slopes(n):
    return 2.0 ** (-8.0 * jnp.arange(1, n + 1, dtype=F32) / n)


def _ffn(h, w_gate, w_up, w_down):
    act = _gate_up(h, w_gate, w_up)
    return _matmul([act], w_down.astype(BF16), FFN_DOWN_TILE, name="ffn_down")


def kernel(x, c, w_ada, b_ada, ffn1_norm_pre, ffn1_norm_post, ffn1_w_gate, ffn1_w_up, ffn1_w_down, mix_norm_pre, mix_norm_post, w_in, attn_sinks, norm_a, norm_b, w_out, ffn2_norm_pre, ffn2_norm_post, ffn2_w_gate, ffn2_w_up, ffn2_w_down):
    bsz, s, d = x.shape
    depth = w_ada.shape[0]
    slopes_a = _alibi_slopes(N_HEADS_A)
    slopes_b = _alibi_slopes(N_HEADS_B)
    for l in range(depth):
        mod = _ada(c, w_ada[l], b_ada[l]).reshape(bsz, N_MOD, 1, d)
        shift, scale, gate = (lambda i: mod[:, 3 * i]), (lambda i: mod[:, 3 * i + 1]), (lambda i: mod[:, 3 * i + 2])

        h = _norm_mod(x, pre=(ffn1_norm_pre[l], scale(0), shift(0)))[0]
        y = _ffn(h.reshape(bsz * s, d), ffn1_w_gate[l], ffn1_w_up[l], ffn1_w_down[l])

        x, h = _norm_mod(x, post=(y.reshape(bsz, s, d), ffn1_norm_post[l], gate(0)),
                         pre=(mix_norm_pre[l], scale(1), shift(1)), res_w=0.5)
        proj = _matmul([h.reshape(bsz * s, d)], w_in[l].astype(BF16), PROJ_TILE, name="w_in")
        proj = proj.reshape(bsz, s, D_IN)
        ya = _swa(proj, attn_sinks[l], slopes_a, norm_a[l])
        yb = _moba(proj, slopes_b, norm_b[l])
        y = _matmul([ya.reshape(bsz * s, WIDTH_A), yb.reshape(bsz * s, WIDTH_B)], w_out[l].astype(BF16),
                    PROJ_TILE, name="w_out")

        x, h = _norm_mod(x, post=(y.reshape(bsz, s, d), mix_norm_post[l], gate(1)),
                         pre=(ffn2_norm_pre[l], scale(2), shift(2)), res_w=1.0)
        y = _ffn(h.reshape(bsz * s, d), ffn2_w_gate[l], ffn2_w_up[l], ffn2_w_down[l])
        x = _norm_mod(x, post=(y.reshape(bsz, s, d), ffn2_norm_post[l], gate(2)), res_w=0.5)[0]
    return x
```

```python
import functools
import math

import jax
import jax.numpy as jnp
from jax import lax
from jax.experimental import pallas as pl
from jax.experimental.pallas import tpu as pltpu

HEAD_DIM = 128
N_HEADS_A = 16
N_KV_A = 2
WINDOW = 128
N_HEADS_B = 16
N_KV_B = 4
MOBA_BLOCK = 256
MOBA_TOPK = 3
EPS = 1e-6
NEG_INF = -1e30
N_MOD = 9

WIDTH_A = N_HEADS_A * HEAD_DIM
WIDTH_B = N_HEADS_B * HEAD_DIM
KV_A = N_KV_A * HEAD_DIM
KV_B = N_KV_B * HEAD_DIM
D_IN = WIDTH_A + 2 * KV_A + WIDTH_B + 2 * KV_B

OFF_QA = 0
OFF_KA = OFF_QA + WIDTH_A
OFF_VA = OFF_KA + KV_A
OFF_QB = OFF_VA + KV_A
OFF_KB = OFF_QB + WIDTH_B
OFF_VB = OFF_KB + KV_B

F32 = jnp.float32
BF16 = jnp.bfloat16
NT_DIMS = (((1,), (1,)), ((), ()))

V7X_VMEM_BYTES = 64 * 1024 * 1024

GATE_UP_TILE = (4096, 256)
FFN_DOWN_TILE = (512, 512)
PROJ_TILE = (2048, 512)
ADA_COLS = 1024
NORM_ROWS = 256


def _nbytes(shape, dtype):
    return math.prod(shape) * jnp.dtype(dtype).itemsize


def _params(n_grid, windows, held=()):
    need = 2 * sum(_nbytes(*w) for w in windows) + sum(_nbytes(*h) for h in held)
    assert need <= V7X_VMEM_BYTES, (need, V7X_VMEM_BYTES)
    return pltpu.CompilerParams(dimension_semantics=("arbitrary",) * n_grid, vmem_limit_bytes=need)


def _rms(v, g):
    return v * lax.rsqrt(jnp.mean(v * v, axis=-1, keepdims=True) + EPS) * g


def _ada_kernel(c_ref, w_ref, b_ref, o_ref):
    c = c_ref[...]
    s = c * jax.nn.sigmoid(c)
    hi = s.astype(BF16)
    lo = (s - hi.astype(F32)).astype(BF16)
    lhs = jnp.concatenate([hi, lo], axis=0)
    r = jnp.dot(lhs, w_ref[...].astype(BF16), preferred_element_type=F32)
    o_ref[...] = r[:8] + r[8:] + b_ref[...]


def _ada(c, w, b):
    bsz, d = c.shape
    n = w.shape[1]
    tn = ADA_COLS
    c8 = jnp.pad(c, ((0, 8 - bsz), (0, 0)))
    out = pl.pallas_call(
        _ada_kernel,
        grid=(n // tn,),
        in_specs=[pl.BlockSpec((8, d), lambda j: (0, 0)),
                  pl.BlockSpec((d, tn), lambda j: (0, j)),
                  pl.BlockSpec((1, tn), lambda j: (0, j))],
        out_specs=pl.BlockSpec((8, tn), lambda j: (0, j)),
        out_shape=jax.ShapeDtypeStruct((8, n), F32),
        compiler_params=_params(1, [((8, d), F32), ((d, tn), F32), ((1, tn), F32), ((8, tn), F32)],
                                held=[((d, tn), BF16)]),
        name="ada",
    )(c8, w, b.reshape(1, n))
    return out[:bsz]


def _norm_mod_kernel(*refs, has_post, has_pre, res_w):
    refs = list(refs)
    x = refs.pop(0)[...]
    if has_post:
        y_ref, gpost_ref, gate_ref = refs.pop(0), refs.pop(0), refs.pop(0)
    if has_pre:
        gpre_ref, scale_ref, shift_ref = refs.pop(0), refs.pop(0), refs.pop(0)
    if has_post:
        x = x + res_w * gate_ref[...] * _rms(y_ref[...].astype(F32), gpost_ref[...])
        refs.pop(0)[...] = x
    if has_pre:
        refs.pop(0)[...] = (_rms(x, gpre_ref[...]) * (1.0 + scale_ref[...]) + shift_ref[...]).astype(BF16)


def _norm_mod(x, post=None, pre=None, res_w=1.0):
    bsz, s, d = x.shape
    ts = NORM_ROWS
    row = pl.BlockSpec((None, ts, d), lambda b, t: (b, t, 0))
    gain = pl.BlockSpec((1, d), lambda b, t: (0, 0))
    mod = pl.BlockSpec((None, 1, d), lambda b, t: (b, 0, 0))
    args, in_specs, out_specs, out_shape = [x], [row], [], []
    if post is not None:
        y, g_post, gate = post
        args += [y, g_post.reshape(1, d), gate]
        in_specs += [row, gain, mod]
        out_specs.append(row)
        out_shape.append(jax.ShapeDtypeStruct((bsz, s, d), F32))
    if pre is not None:
        g_pre, scale, shift = pre
        args += [g_pre.reshape(1, d), scale, shift]
        in_specs += [gain, mod, mod]
        out_specs.append(row)
        out_shape.append(jax.ShapeDtypeStruct((bsz, s, d), BF16))
    return pl.pallas_call(
        functools.partial(_norm_mod_kernel, has_post=post is not None, has_pre=pre is not None, res_w=res_w),
        grid=(bsz, s // ts),
        in_specs=in_specs, out_specs=out_specs, out_shape=out_shape,
        compiler_params=_params(2, [((ts, d), F32)] * 2 + [((ts, d), BF16)] * 2 + [((1, d), F32)] * 5,
                                held=[((ts, d), F32)] * 4),
        name="norm_mod",
    )(*args)


def _gate_up_kernel(h_ref, wg_ref, wu_ref, o_ref):
    h = h_ref[...]
    g = jnp.dot(h, wg_ref[...].astype(BF16), preferred_element_type=F32)
    u = jnp.dot(h, wu_ref[...].astype(BF16), preferred_element_type=F32)
    o_ref[...] = (g * jax.nn.sigmoid(g) * u).astype(o_ref.dtype)


def _gate_up(h, wg, wu):
    m, k = h.shape
    f = wg.shape[1]
    tm, tf = GATE_UP_TILE
    return pl.pallas_call(
        _gate_up_kernel,
        grid=(m // tm, f // tf),
        in_specs=[pl.BlockSpec((tm, k), lambda i, j: (i, 0), pipeline_mode=pl.Buffered(1)),
                  pl.BlockSpec((k, tf), lambda i, j: (0, j)),
                  pl.BlockSpec((k, tf), lambda i, j: (0, j))],
        out_specs=pl.BlockSpec((tm, tf), lambda i, j: (i, j)),
        out_shape=jax.ShapeDtypeStruct((m, f), BF16),
        compiler_params=_params(2, [((k, tf), F32), ((k, tf), F32), ((tm, tf), BF16)],
                                held=[((tm, k), BF16)] + [((k, tf), BF16)] * 2 + [((tm, tf), F32)] * 2),
        name="gate_up",
    )(h, wg, wu)


def _mm_kernel(*refs, splits):
    w_ref, o_ref = refs[len(splits)], refs[-1]
    acc, off = None, 0
    for a_ref, k in zip(refs, splits):
        part = jnp.dot(a_ref[...], w_ref[off:off + k, :], preferred_element_type=F32)
        acc = part if acc is None else acc + part
        off += k
    o_ref[...] = acc.astype(o_ref.dtype)


def _matmul(a_parts, w, tile, name):
    tm, tn = tile
    m = a_parts[0].shape[0]
    n = w.shape[1]
    splits = tuple(a.shape[1] for a in a_parts)
    return pl.pallas_call(
        functools.partial(_mm_kernel, splits=splits),
        grid=(m // tm, n // tn),
        in_specs=[pl.BlockSpec((tm, k), lambda i, j: (i, 0)) for k in splits]
        + [pl.BlockSpec((sum(splits), tn), lambda i, j: (0, j))],
        out_specs=pl.BlockSpec((tm, tn), lambda i, j: (i, j)),
        out_shape=jax.ShapeDtypeStruct((m, n), BF16),
        compiler_params=_params(2, [((tm, k), BF16) for k in splits] + [((sum(splits), tn), BF16), ((tm, tn), BF16)],
                                held=[((tm, tn), F32)] * 2),
        name=name,
    )(*a_parts, w)


SWA_TQ = 1024
SWA_GROUP = N_HEADS_A // N_KV_A


def _swa_kernel(sink_ref, slope_ref, q_ref, kown_ref, kprev_ref, vtown_ref, vtprev_ref, g_ref, o_ref,
                bias_scr, o_scr):
    b, t = pl.program_id(0), pl.program_id(1)
    to_log2 = (HEAD_DIM ** -0.5) * math.log2(math.e)
    keys, lanes = 2 * WINDOW, SWA_GROUP * WINDOW

    @pl.when((b == 0) & (t == 0))
    def _():
        key = lax.broadcasted_iota(jnp.int32, (keys, lanes), 0)
        qry = lax.broadcasted_iota(jnp.int32, (keys, lanes), 1) & (WINDOW - 1)
        dist = qry + WINDOW - key
        inside = (dist >= 0) & (dist < WINDOW)
        distf = dist.astype(F32) * math.log2(math.e)
        for j in range(N_KV_A):
            slope = jnp.concatenate(
                [jnp.full((keys, WINDOW), slope_ref[j * SWA_GROUP + g], F32) for g in range(SWA_GROUP)], axis=1)
            bias_scr[j] = jnp.where(inside, -(slope * distf), NEG_INF)

    first_key = jnp.where(t > 0, 0, WINDOW)
    no_prev = jnp.where(lax.broadcasted_iota(jnp.int32, (keys, lanes), 0) >= first_key, 0.0, NEG_INF)

    for j in range(N_KV_A):
        kvc = slice(j * HEAD_DIM, (j + 1) * HEAD_DIM)
        sink2 = jnp.concatenate(
            [jnp.full((1, WINDOW), sink_ref[j * SWA_GROUP + g], F32) for g in range(SWA_GROUP)], axis=1
        ) * math.log2(math.e)
        for s in range(SWA_TQ // WINDOW):
            rows = slice(s * WINDOW, (s + 1) * WINDOW)
            if s == 0:
                kk = jnp.concatenate([kprev_ref[:, kvc], kown_ref[0:WINDOW, kvc]], axis=0)
                vv_t = jnp.concatenate([vtprev_ref[kvc, :], vtown_ref[kvc, 0:WINDOW]], axis=1)
            else:
                kk = kown_ref[(s - 1) * WINDOW:(s + 1) * WINDOW, kvc]
                vv_t = vtown_ref[kvc, (s - 1) * WINDOW:(s + 1) * WINDOW]
            qs = jnp.concatenate(
                [q_ref[rows, (j * SWA_GROUP + g) * HEAD_DIM:(j * SWA_GROUP + g + 1) * HEAD_DIM]
                 for g in range(SWA_GROUP)], axis=0)
            sc = lax.dot_general(kk, qs, NT_DIMS, preferred_element_type=F32) * to_log2 + bias_scr[j]
            if s == 0:
                sc = sc + no_prev
            mx = jnp.maximum(jnp.max(sc, axis=0, keepdims=True), sink2)
            p = jnp.exp2(sc - mx)
            denom = jnp.sum(p, axis=0, keepdims=True) + jnp.exp2(sink2 - mx)
            o_t = jnp.dot(vv_t, p.astype(BF16), preferred_element_type=F32) / denom
            for g in range(SWA_GROUP):
                h = j * SWA_GROUP + g
                o_scr[rows, h * HEAD_DIM:(h + 1) * HEAD_DIM] = o_t[:, g * WINDOW:(g + 1) * WINDOW].T
    o_ref[...] = _rms(o_scr[...], g_ref[...]).astype(o_ref.dtype)


def _swa(proj, sinks, slopes, norm_a):
    bsz, s, _ = proj.shape
    wpt = SWA_TQ // WINDOW
    v_t = proj[:, :, OFF_VA:OFF_VA + KV_A].transpose(0, 2, 1)
    smem = pl.BlockSpec(memory_space=pltpu.SMEM)
    return pl.pallas_call(
        _swa_kernel,
        grid=(bsz, s // SWA_TQ),
        in_specs=[smem, smem,
                  pl.BlockSpec((None, SWA_TQ, WIDTH_A), lambda b, t: (b, t, OFF_QA // WIDTH_A)),
                  pl.BlockSpec((None, SWA_TQ, KV_A), lambda b, t: (b, t, OFF_KA // KV_A)),
                  pl.BlockSpec((None, WINDOW, KV_A), lambda b, t: (b, jnp.maximum(t * wpt - 1, 0), OFF_KA // KV_A)),
                  pl.BlockSpec((None, KV_A, SWA_TQ), lambda b, t: (b, 0, t)),
                  pl.BlockSpec((None, KV_A, WINDOW), lambda b, t: (b, 0, jnp.maximum(t * wpt - 1, 0))),
                  pl.BlockSpec((1, WIDTH_A), lambda b, t: (0, 0))],
        out_specs=pl.BlockSpec((None, SWA_TQ, WIDTH_A), lambda b, t: (b, t, 0)),
        out_shape=jax.ShapeDtypeStruct((bsz, s, WIDTH_A), BF16),
        scratch_shapes=[pltpu.VMEM((N_KV_A, 2 * WINDOW, SWA_GROUP * WINDOW), F32),
                        pltpu.VMEM((SWA_TQ, WIDTH_A), F32)],
        compiler_params=_params(
            2, [((SWA_TQ, WIDTH_A), BF16)] * 2 + [((SWA_TQ, KV_A), BF16)] * 2 + [((WINDOW, KV_A), BF16)] * 2
            + [((1, WIDTH_A), F32)],
            held=[((N_KV_A, 2 * WINDOW, SWA_GROUP * WINDOW), F32), ((SWA_TQ, WIDTH_A), F32)]
            + [((2 * WINDOW, SWA_GROUP * WINDOW), F32)] * 4 + [((SWA_TQ, WIDTH_A), F32)]),
        name="swa",
    )(sinks, slopes, proj, proj, proj, v_t, v_t, norm_a.reshape(1, WIDTH_A))


MOBA_GROUP = N_HEADS_B // N_KV_B
N_ALIBI_TERMS = 3


def _moba_kernel(slope_ref, q0_ref, q1_ref, q2_ref, q3_ref, k_ref, vt_ref, g_ref, o_ref,
                 kmean_scr, qext_scr, bias_scr, m_scr, l_scr, acc_scr, y_scr):
    q_refs = (q0_ref, q1_ref, q2_ref, q3_ref)
    group_cols = [slice(g * HEAD_DIM, (g + 1) * HEAD_DIM) for g in range(MOBA_GROUP)]
    n = pl.program_id(1)
    blk_sz = MOBA_BLOCK
    nblk = k_ref.shape[0] // blk_sz
    qrows = MOBA_GROUP * blk_sz
    scale = HEAD_DIM ** -0.5
    to_log2 = scale * math.log2(math.e)

    @pl.when(n == 0)
    def _():
        kmean = jnp.concatenate(
            [jnp.sum(k_ref[i * blk_sz:(i + 1) * blk_sz, :].astype(F32), axis=0, keepdims=True)
             for i in range(nblk)], axis=0) * (1.0 / blk_sz)
        hi = kmean.astype(BF16)
        lo = (kmean - hi.astype(F32)).astype(BF16)
        for j in range(N_KV_B):
            kc = slice(j * HEAD_DIM, (j + 1) * HEAD_DIM)
            kmean_scr[j] = jnp.concatenate([hi[:, kc], lo[:, kc]], axis=0)
        f_lane = lax.broadcasted_iota(jnp.int32, (blk_sz, HEAD_DIM), 1)
        for h in range(N_HEADS_B):
            a = jnp.full((blk_sz, HEAD_DIM), slope_ref[h], F32) * (1.0 / scale)
            a1 = a.astype(BF16).astype(F32)
            a2 = (a - a1).astype(BF16).astype(F32)
            a3 = (a - a1 - a2).astype(BF16).astype(F32)
            term = jnp.where((f_lane == 0) | (f_lane == 3), a1, jnp.where((f_lane == 1) | (f_lane == 4), a2, a3))
            qext_scr[h, :, HEAD_DIM:2 * HEAD_DIM] = jnp.where(f_lane < 2 * N_ALIBI_TERMS, term, 0.0).astype(BF16)

    lane = lax.broadcasted_iota(jnp.int32, (blk_sz, HEAD_DIM), 1)
    key_off = lax.broadcasted_iota(jnp.int32, (blk_sz, HEAD_DIM), 0).astype(F32)
    off_lanes = lane < N_ALIBI_TERMS
    far_lanes = (lane >= N_ALIBI_TERMS) & (lane < 2 * N_ALIBI_TERMS)
    own_feat = jnp.where(off_lanes, key_off, 0.0).astype(BF16)
    causal = jnp.where(lax.broadcasted_iota(jnp.int32, (blk_sz, blk_sz), 0)
                       <= lax.broadcasted_iota(jnp.int32, (blk_sz, blk_sz), 1), 0.0, NEG_INF)
    blk = lax.broadcasted_iota(jnp.int32, (nblk, qrows), 0)

    head_cols = [slice(h * HEAD_DIM, (h + 1) * HEAD_DIM) for h in range(N_HEADS_B)]
    kv_cols = [slice(j * HEAD_DIM, (j + 1) * HEAD_DIM) for j in range(N_KV_B)]

    for j in range(N_KV_B):
        qs = jnp.concatenate([q_refs[j][:, gc] for gc in group_cols], axis=0)
        g2 = lax.dot_general(kmean_scr[j], qs, NT_DIMS, preferred_element_type=F32)
        gate = g2[:nblk] + g2[nblk:]
        beaten = jnp.zeros((nblk, qrows), F32)
        for i in range(nblk):
            gi = gate[i:i + 1, :]
            beats = (gi > gate) | ((gi == gate) & (blk > i))
            beaten = beaten + jnp.where(beats, jnp.where(i < n, 1.0, 0.0), 0.0)
        bias_t = jnp.where((blk < n) & (beaten < MOBA_TOPK), 0.0, NEG_INF)

        for g in range(MOBA_GROUP):
            h = j * MOBA_GROUP + g
            bias_scr[h] = bias_t[:, g * blk_sz:(g + 1) * blk_sz]
            qext_scr[h, :, 0:HEAD_DIM] = q_refs[j][:, group_cols[g]]

    def attend(i, own):
        st = pl.multiple_of(i * blk_sz, blk_sz)
        if own:
            kfeat = own_feat
        else:
            far = ((n - i) * blk_sz).astype(F32)
            kfeat = jnp.where(far_lanes, -far, jnp.where(off_lanes, key_off, 0.0)).astype(BF16)
        logits = []
        for j in range(N_KV_B):
            k_ext = jnp.concatenate([k_ref[pl.ds(st, blk_sz), kv_cols[j]], kfeat], axis=1)
            for g in range(MOBA_GROUP):
                logits.append(lax.dot_general(k_ext, qext_scr[j * MOBA_GROUP + g], NT_DIMS,
                                              preferred_element_type=F32))
        for h in range(N_HEADS_B):
            t = logits[h] * to_log2
            if own:
                t = t + causal
                m_new = jnp.max(t, axis=0, keepdims=True)
                p = jnp.exp2(t - m_new)
            else:
                bias = bias_scr[h, pl.ds(i, 1), :]
                m_old = m_scr[h][0:1, :]
                m_new = jnp.maximum(m_old, jnp.max(logits[h], axis=0, keepdims=True) * to_log2 + bias)
                p = jnp.exp2(t + (bias - m_new))
            l_new = jnp.sum(p, axis=0, keepdims=True)
            acc = jnp.dot(vt_ref[i, kv_cols[h // MOBA_GROUP], :], p.astype(BF16), preferred_element_type=F32)
            if not own:
                alpha = jnp.exp2(m_old - m_new)
                l_new = alpha * l_scr[h][0:1, :] + l_new
                acc = alpha * acc_scr[h] + acc
            m_scr[h] = jnp.broadcast_to(m_new, (8, blk_sz))
            l_scr[h] = jnp.broadcast_to(l_new, (8, blk_sz))
            acc_scr[h] = acc

    attend(n, own=True)

    def past_block(i, carry):
        attend(i, own=False)
        return carry

    lax.fori_loop(0, n, past_block, 0)

    for h in range(N_HEADS_B):
        y_scr[:, head_cols[h]] = (acc_scr[h] / l_scr[h][0:1, :]).T
    o_ref[...] = _rms(y_scr[...], g_ref[...]).astype(o_ref.dtype)


def _moba(proj, slopes, norm_b):
    bsz, s, _ = proj.shape
    nblk = s // MOBA_BLOCK
    v_t = proj[:, :, OFF_VB:OFF_VB + KV_B].reshape(bsz, nblk, MOBA_BLOCK, KV_B).transpose(0, 1, 3, 2)
    q_width = MOBA_GROUP * HEAD_DIM

    def q_block(j, b, n):
        return (b, n, OFF_QB // q_width + j)

    return pl.pallas_call(
        _moba_kernel,
        grid=(bsz, nblk),
        in_specs=[pl.BlockSpec(memory_space=pltpu.SMEM),
                  *[pl.BlockSpec((None, MOBA_BLOCK, q_width), functools.partial(q_block, j))
                    for j in range(N_KV_B)],
                  pl.BlockSpec((None, s, KV_B), lambda b, n: (b, 0, OFF_KB // KV_B)),
                  pl.BlockSpec((None, nblk, KV_B, MOBA_BLOCK), lambda b, n: (b, 0, 0, 0)),
                  pl.BlockSpec((1, WIDTH_B), lambda b, n: (0, 0))],
        out_specs=pl.BlockSpec((None, MOBA_BLOCK, WIDTH_B), lambda b, n: (b, n, 0)),
        out_shape=jax.ShapeDtypeStruct((bsz, s, WIDTH_B), BF16),
        scratch_shapes=[pltpu.VMEM((N_KV_B, 2 * nblk, HEAD_DIM), BF16),
                        pltpu.VMEM((N_HEADS_B, MOBA_BLOCK, 2 * HEAD_DIM), BF16),
                        pltpu.VMEM((N_HEADS_B, nblk, MOBA_BLOCK), F32),
                        pltpu.VMEM((N_HEADS_B, 8, MOBA_BLOCK), F32),
                        pltpu.VMEM((N_HEADS_B, 8, MOBA_BLOCK), F32),
                        pltpu.VMEM((N_HEADS_B, HEAD_DIM, MOBA_BLOCK), F32),
                        pltpu.VMEM((MOBA_BLOCK, WIDTH_B), F32)],
        compiler_params=_params(
            2, [((MOBA_BLOCK, WIDTH_B), BF16)] * 2 + [((s, KV_B), BF16)] * 2 + [((1, WIDTH_B), F32)],
            held=[((N_HEADS_B, MOBA_BLOCK, 2 * HEAD_DIM), BF16), ((N_HEADS_B, nblk + 16, MOBA_BLOCK), F32),
                  ((N_HEADS_B, HEAD_DIM, MOBA_BLOCK), F32), ((MOBA_BLOCK, WIDTH_B), F32)]
            + [((N_HEADS_B, MOBA_BLOCK, MOBA_BLOCK), F32)] * 2 + [((MOBA_BLOCK, WIDTH_B), F32)]),
        name="moba",
    )(slopes, *([proj] * N_KV_B), proj, v_t, norm_b.reshape(1, WIDTH_B))


def _alibi_slopes(n):
    return 2.0 ** (-8.0 * jnp.arange(1, n + 1, dtype=F32) / n)


def _ffn(h, w_gate, w_up, w_down):
    act = _gate_up(h, w_gate, w_up)
    return _matmul([act], w_down.astype(BF16), FFN_DOWN_TILE, name="ffn_down")


def kernel(x, c, w_ada, b_ada, ffn1_norm_pre, ffn1_norm_post, ffn1_w_gate, ffn1_w_up, ffn1_w_down, mix_norm_pre, mix_norm_post, w_in, attn_sinks, norm_a, norm_b, w_out, ffn2_norm_pre, ffn2_norm_post, ffn2_w_gate, ffn2_w_up, ffn2_w_down):
    bsz, s, d = x.shape
    depth = w_ada.shape[0]
    slopes_a = _alibi_slopes(N_HEADS_A)
    slopes_b = _alibi_slopes(N_HEADS_B)
    for l in range(depth):
        mod = _ada(c, w_ada[l], b_ada[l]).reshape(bsz, N_MOD, 1, d)
        shift, scale, gate = (lambda i: mod[:, 3 * i]), (lambda i: mod[:, 3 * i + 1]), (lambda i: mod[:, 3 * i + 2])

        h = _norm_mod(x, pre=(ffn1_norm_pre[l], scale(0), shift(0)))[0]
        y = _ffn(h.reshape(bsz * s, d), ffn1_w_gate[l], ffn1_w_up[l], ffn1_w_down[l])

        x, h = _norm_mod(x, post=(y.reshape(bsz, s, d), ffn1_norm_post[l], gate(0)),
                         pre=(mix_norm_pre[l], scale(1), shift(1)), res_w=0.5)
        proj = _matmul([h.reshape(bsz * s, d)], w_in[l].astype(BF16), PROJ_TILE, name="w_in")
        proj = proj.reshape(bsz, s, D_IN)
        ya = _swa(proj, attn_sinks[l], slopes_a, norm_a[l])
        yb = _moba(proj, slopes_b, norm_b[l])
        y = _matmul([ya.reshape(bsz * s, WIDTH_A), yb.reshape(bsz * s, WIDTH_B)], w_out[l].astype(BF16),
                    PROJ_TILE, name="w_out")

        x, h = _norm_mod(x, post=(y.reshape(bsz, s, d), mix_norm_post[l], gate(1)),
                         pre=(ffn2_norm_pre[l], scale(2), shift(2)), res_w=1.0)
        y = _ffn(h.reshape(bsz * s, d), ffn2_w_gate[l], ffn2_w_up[l], ffn2_w_down[l])
        x = _norm_mod(x, post=(y.reshape(bsz, s, d), ffn2_norm_post[l], gate(2)), res_w=0.5)[0]
    return x
```

```python
import functools
import math

import jax
import jax.numpy as jnp
from jax import lax
from jax.experimental import pallas as pl
from jax.experimental.pallas import tpu as pltpu

HEAD_DIM = 128
N_HEADS_A = 16
N_KV_A = 2
WINDOW = 128
N_HEADS_B = 16
N_KV_B = 4
MOBA_BLOCK = 256
MOBA_TOPK = 3
EPS = 1e-6
NEG_INF = -1e30
N_MOD = 9

WIDTH_A = N_HEADS_A * HEAD_DIM
WIDTH_B = N_HEADS_B * HEAD_DIM
KV_A = N_KV_A * HEAD_DIM
KV_B = N_KV_B * HEAD_DIM
D_IN = WIDTH_A + 2 * KV_A + WIDTH_B + 2 * KV_B

OFF_QA = 0
OFF_KA = OFF_QA + WIDTH_A
OFF_VA = OFF_KA + KV_A
OFF_QB = OFF_VA + KV_A
OFF_KB = OFF_QB + WIDTH_B
OFF_VB = OFF_KB + KV_B

F32 = jnp.float32
BF16 = jnp.bfloat16
NT_DIMS = (((1,), (1,)), ((), ()))

V7X_VMEM_BYTES = 64 * 1024 * 1024

GATE_UP_TILE = (2048, 256)
FFN_DOWN_TILE = (512, 512)
PROJ_TILE = (2048, 512)
ADA_COLS = 1024
NORM_ROWS = 256


def _nbytes(shape, dtype):
    return math.prod(shape) * jnp.dtype(dtype).itemsize


def _params(n_grid, windows, held=()):
    need = 2 * sum(_nbytes(*w) for w in windows) + sum(_nbytes(*h) for h in held)
    assert need <= V7X_VMEM_BYTES, (need, V7X_VMEM_BYTES)
    return pltpu.CompilerParams(dimension_semantics=("arbitrary",) * n_grid, vmem_limit_bytes=need)


def _rms(v, g):
    return v * lax.rsqrt(jnp.mean(v * v, axis=-1, keepdims=True) + EPS) * g


def _ada_kernel(c_ref, w_ref, b_ref, o_ref):
    c = c_ref[...]
    s = c * jax.nn.sigmoid(c)
    hi = s.astype(BF16)
    lo = (s - hi.astype(F32)).astype(BF16)
    lhs = jnp.concatenate([hi, lo], axis=0)
    r = jnp.dot(lhs, w_ref[...].astype(BF16), preferred_element_type=F32)
    o_ref[...] = r[:8] + r[8:] + b_ref[...]


def _ada(c, w, b):
    bsz, d = c.shape
    n = w.shape[1]
    tn = ADA_COLS
    c8 = jnp.pad(c, ((0, 8 - bsz), (0, 0)))
    out = pl.pallas_call(
        _ada_kernel,
        grid=(n // tn,),
        in_specs=[pl.BlockSpec((8, d), lambda j: (0, 0)),
                  pl.BlockSpec((d, tn), lambda j: (0, j)),
                  pl.BlockSpec((1, tn), lambda j: (0, j))],
        out_specs=pl.BlockSpec((8, tn), lambda j: (0, j)),
        out_shape=jax.ShapeDtypeStruct((8, n), F32),
        compiler_params=_params(1, [((8, d), F32), ((d, tn), F32), ((1, tn), F32), ((8, tn), F32)],
                                held=[((d, tn), BF16)]),
        name="ada",
    )(c8, w, b.reshape(1, n))
    return out[:bsz]


def _norm_mod_kernel(*refs, has_post, has_pre, res_w):
    refs = list(refs)
    x = refs.pop(0)[...]
    if has_post:
        y_ref, gpost_ref, gate_ref = refs.pop(0), refs.pop(0), refs.pop(0)
    if has_pre:
        gpre_ref, scale_ref, shift_ref = refs.pop(0), refs.pop(0), refs.pop(0)
    if has_post:
        x = x + res_w * gate_ref[...] * _rms(y_ref[...].astype(F32), gpost_ref[...])
        refs.pop(0)[...] = x
    if has_pre:
        refs.pop(0)[...] = (_rms(x, gpre_ref[...]) * (1.0 + scale_ref[...]) + shift_ref[...]).astype(BF16)


def _norm_mod(x, post=None, pre=None, res_w=1.0):
    bsz, s, d = x.shape
    ts = NORM_ROWS
    row = pl.BlockSpec((None, ts, d), lambda b, t: (b, t, 0))
    gain = pl.BlockSpec((1, d), lambda b, t: (0, 0))
    mod = pl.BlockSpec((None, 1, d), lambda b, t: (b, 0, 0))
    args, in_specs, out_specs, out_shape = [x], [row], [], []
    if post is not None:
        y, g_post, gate = post
        args += [y, g_post.reshape(1, d), gate]
        in_specs += [row, gain, mod]
        out_specs.append(row)
        out_shape.append(jax.ShapeDtypeStruct((bsz, s, d), F32))
    if pre is not None:
        g_pre, scale, shift = pre
        args += [g_pre.reshape(1, d), scale, shift]
        in_specs += [gain, mod, mod]
        out_specs.append(row)
        out_shape.append(jax.ShapeDtypeStruct((bsz, s, d), BF16))
    return pl.pallas_call(
        functools.partial(_norm_mod_kernel, has_post=post is not None, has_pre=pre is not None, res_w=res_w),
        grid=(bsz, s // ts),
        in_specs=in_specs, out_specs=out_specs, out_shape=out_shape,
        compiler_params=_params(2, [((ts, d), F32)] * 2 + [((ts, d), BF16)] * 2 + [((1, d), F32)] * 5,
                                held=[((ts, d), F32)] * 4),
        name="norm_mod",
    )(*args)


def _gate_up_kernel(h_ref, wg_ref, wu_ref, o_ref):
    h = h_ref[...]
    g = jnp.dot(h, wg_ref[...].astype(BF16), preferred_element_type=F32)
    u = jnp.dot(h, wu_ref[...].astype(BF16), preferred_element_type=F32)
    o_ref[...] = (g * jax.nn.sigmoid(g) * u).astype(o_ref.dtype)


def _gate_up(h, wg, wu):
    m, k = h.shape
    f = wg.shape[1]
    tm, tf = GATE_UP_TILE
    return pl.pallas_call(
        _gate_up_kernel,
        grid=(m // tm, f // tf),
        in_specs=[pl.BlockSpec((tm, k), lambda i, j: (i, 0)),
                  pl.BlockSpec((k, tf), lambda i, j: (0, j)),
                  pl.BlockSpec((k, tf), lambda i, j: (0, j))],
        out_specs=pl.BlockSpec((tm, tf), lambda i, j: (i, j)),
        out_shape=jax.ShapeDtypeStruct((m, f), BF16),
        compiler_params=_params(2, [((tm, k), BF16), ((k, tf), F32), ((k, tf), F32), ((tm, tf), BF16)],
                                held=[((k, tf), BF16)] * 2 + [((tm, tf), F32)] * 2),
        name="gate_up",
    )(h, wg, wu)


def _mm_kernel(*refs, splits):
    w_ref, o_ref = refs[len(splits)], refs[-1]
    acc, off = None, 0
    for a_ref, k in zip(refs, splits):
        part = jnp.dot(a_ref[...], w_ref[off:off + k, :], preferred_element_type=F32)
        acc = part if acc is None else acc + part
        off += k
    o_ref[...] = acc.astype(o_ref.dtype)


def _matmul(a_parts, w, tile, name):
    tm, tn = tile
    m = a_parts[0].shape[0]
    n = w.shape[1]
    splits = tuple(a.shape[1] for a in a_parts)
    return pl.pallas_call(
        functools.partial(_mm_kernel, splits=splits),
        grid=(m // tm, n // tn),
        in_specs=[pl.BlockSpec((tm, k), lambda i, j: (i, 0)) for k in splits]
        + [pl.BlockSpec((sum(splits), tn), lambda i, j: (0, j))],
        out_specs=pl.BlockSpec((tm, tn), lambda i, j: (i, j)),
        out_shape=jax.ShapeDtypeStruct((m, n), BF16),
        compiler_params=_params(2, [((tm, k), BF16) for k in splits] + [((sum(splits), tn), BF16), ((tm, tn), BF16)],
                                held=[((tm, tn), F32)] * 2),
        name=name,
    )(*a_parts, w)


SWA_TQ = 1024
SWA_GROUP = N_HEADS_A // N_KV_A


def _swa_kernel(sink_ref, slope_ref, q_ref, kown_ref, kprev_ref, vtown_ref, vtprev_ref, g_ref, o_ref,
                bias_scr, o_scr):
    b, t = pl.program_id(0), pl.program_id(1)
    to_log2 = (HEAD_DIM ** -0.5) * math.log2(math.e)
    keys, lanes = 2 * WINDOW, SWA_GROUP * WINDOW

    @pl.when((b == 0) & (t == 0))
    def _():
        key = lax.broadcasted_iota(jnp.int32, (keys, lanes), 0)
        qry = lax.broadcasted_iota(jnp.int32, (keys, lanes), 1) & (WINDOW - 1)
        dist = qry + WINDOW - key
        inside = (dist >= 0) & (dist < WINDOW)
        distf = dist.astype(F32) * math.log2(math.e)
        for j in range(N_KV_A):
            slope = jnp.concatenate(
                [jnp.full((keys, WINDOW), slope_ref[j * SWA_GROUP + g], F32) for g in range(SWA_GROUP)], axis=1)
            bias_scr[j] = jnp.where(inside, -(slope * distf), NEG_INF)

    first_key = jnp.where(t > 0, 0, WINDOW)
    no_prev = jnp.where(lax.broadcasted_iota(jnp.int32, (keys, lanes), 0) >= first_key, 0.0, NEG_INF)

    for j in range(N_KV_A):
        kvc = slice(j * HEAD_DIM, (j + 1) * HEAD_DIM)
        sink2 = jnp.concatenate(
            [jnp.full((1, WINDOW), sink_ref[j * SWA_GROUP + g], F32) for g in range(SWA_GROUP)], axis=1
        ) * math.log2(math.e)
        for s in range(SWA_TQ // WINDOW):
            rows = slice(s * WINDOW, (s + 1) * WINDOW)
            if s == 0:
                kk = jnp.concatenate([kprev_ref[:, kvc], kown_ref[0:WINDOW, kvc]], axis=0)
                vv_t = jnp.concatenate([vtprev_ref[kvc, :], vtown_ref[kvc, 0:WINDOW]], axis=1)
            else:
                kk = kown_ref[(s - 1) * WINDOW:(s + 1) * WINDOW, kvc]
                vv_t = vtown_ref[kvc, (s - 1) * WINDOW:(s + 1) * WINDOW]
            qs = jnp.concatenate(
                [q_ref[rows, (j * SWA_GROUP + g) * HEAD_DIM:(j * SWA_GROUP + g + 1) * HEAD_DIM]
                 for g in range(SWA_GROUP)], axis=0)
            sc = lax.dot_general(kk, qs, NT_DIMS, preferred_element_type=F32) * to_log2 + bias_scr[j]
            if s == 0:
                sc = sc + no_prev
            mx = jnp.maximum(jnp.max(sc, axis=0, keepdims=True), sink2)
            p = jnp.exp2(sc - mx)
            denom = jnp.sum(p, axis=0, keepdims=True) + jnp.exp2(sink2 - mx)
            o_t = jnp.dot(vv_t, p.astype(BF16), preferred_element_type=F32) / denom
            for g in range(SWA_GROUP):
                h = j * SWA_GROUP + g
                o_scr[rows, h * HEAD_DIM:(h + 1) * HEAD_DIM] = o_t[:, g * WINDOW:(g + 1) * WINDOW].T
    o_ref[...] = _rms(o_scr[...], g_ref[...]).astype(o_ref.dtype)


def _swa(proj, sinks, slopes, norm_a):
    bsz, s, _ = proj.shape
    wpt = SWA_TQ // WINDOW
    v_t = proj[:, :, OFF_VA:OFF_VA + KV_A].transpose(0, 2, 1)
    smem = pl.BlockSpec(memory_space=pltpu.SMEM)
    return pl.pallas_call(
        _swa_kernel,
        grid=(bsz, s // SWA_TQ),
        in_specs=[smem, smem,
                  pl.BlockSpec((None, SWA_TQ, WIDTH_A), lambda b, t: (b, t, OFF_QA // WIDTH_A)),
                  pl.BlockSpec((None, SWA_TQ, KV_A), lambda b, t: (b, t, OFF_KA // KV_A)),
                  pl.BlockSpec((None, WINDOW, KV_A), lambda b, t: (b, jnp.maximum(t * wpt - 1, 0), OFF_KA // KV_A)),
                  pl.BlockSpec((None, KV_A, SWA_TQ), lambda b, t: (b, 0, t)),
                  pl.BlockSpec((None, KV_A, WINDOW), lambda b, t: (b, 0, jnp.maximum(t * wpt - 1, 0))),
                  pl.BlockSpec((1, WIDTH_A), lambda b, t: (0, 0))],
        out_specs=pl.BlockSpec((None, SWA_TQ, WIDTH_A), lambda b, t: (b, t, 0)),
        out_shape=jax.ShapeDtypeStruct((bsz, s, WIDTH_A), BF16),
        scratch_shapes=[pltpu.VMEM((N_KV_A, 2 * WINDOW, SWA_GROUP * WINDOW), F32),
                        pltpu.VMEM((SWA_TQ, WIDTH_A), F32)],
        compiler_params=_params(
            2, [((SWA_TQ, WIDTH_A), BF16)] * 2 + [((SWA_TQ, KV_A), BF16)] * 2 + [((WINDOW, KV_A), BF16)] * 2
            + [((1, WIDTH_A), F32)],
            held=[((N_KV_A, 2 * WINDOW, SWA_GROUP * WINDOW), F32), ((SWA_TQ, WIDTH_A), F32)]
            + [((2 * WINDOW, SWA_GROUP * WINDOW), F32)] * 4 + [((SWA_TQ, WIDTH_A), F32)]),
        name="swa",
    )(sinks, slopes, proj, proj, proj, v_t, v_t, norm_a.reshape(1, WIDTH_A))


MOBA_GROUP = N_HEADS_B // N_KV_B
N_ALIBI_TERMS = 3


def _moba_kernel(slope_ref, q0_ref, q1_ref, q2_ref, q3_ref, k_ref, vt_ref, g_ref, o_ref,
                 kmean_scr, qext_scr, bias_scr, m_scr, l_scr, acc_scr, y_scr):
    q_refs = (q0_ref, q1_ref, q2_ref, q3_ref)
    group_cols = [slice(g * HEAD_DIM, (g + 1) * HEAD_DIM) for g in range(MOBA_GROUP)]
    n = pl.program_id(1)
    blk_sz = MOBA_BLOCK
    nblk = k_ref.shape[0] // blk_sz
    qrows = MOBA_GROUP * blk_sz
    scale = HEAD_DIM ** -0.5
    to_log2 = scale * math.log2(math.e)

    @pl.when(n == 0)
    def _():
        kmean = jnp.concatenate(
            [jnp.sum(k_ref[i * blk_sz:(i + 1) * blk_sz, :].astype(F32), axis=0, keepdims=True)
             for i in range(nblk)], axis=0) * (1.0 / blk_sz)
        hi = kmean.astype(BF16)
        lo = (kmean - hi.astype(F32)).astype(BF16)
        for j in range(N_KV_B):
            kc = slice(j * HEAD_DIM, (j + 1) * HEAD_DIM)
            kmean_scr[j] = jnp.concatenate([hi[:, kc], lo[:, kc]], axis=0)
        f_lane = lax.broadcasted_iota(jnp.int32, (blk_sz, HEAD_DIM), 1)
        for h in range(N_HEADS_B):
            a = jnp.full((blk_sz, HEAD_DIM), slope_ref[h], F32) * (1.0 / scale)
            a1 = a.astype(BF16).astype(F32)
            a2 = (a - a1).astype(BF16).astype(F32)
            a3 = (a - a1 - a2).astype(BF16).astype(F32)
            term = jnp.where((f_lane == 0) | (f_lane == 3), a1, jnp.where((f_lane == 1) | (f_lane == 4), a2, a3))
            qext_scr[h, :, HEAD_DIM:2 * HEAD_DIM] = jnp.where(f_lane < 2 * N_ALIBI_TERMS, term, 0.0).astype(BF16)

    lane = lax.broadcasted_iota(jnp.int32, (blk_sz, HEAD_DIM), 1)
    key_off = lax.broadcasted_iota(jnp.int32, (blk_sz, HEAD_DIM), 0).astype(F32)
    off_lanes = lane < N_ALIBI_TERMS
    far_lanes = (lane >= N_ALIBI_TERMS) & (lane < 2 * N_ALIBI_TERMS)
    own_feat = jnp.where(off_lanes, key_off, 0.0).astype(BF16)
    causal = jnp.where(lax.broadcasted_iota(jnp.int32, (blk_sz, blk_sz), 0)
                       <= lax.broadcasted_iota(jnp.int32, (blk_sz, blk_sz), 1), 0.0, NEG_INF)
    blk = lax.broadcasted_iota(jnp.int32, (nblk, qrows), 0)

    head_cols = [slice(h * HEAD_DIM, (h + 1) * HEAD_DIM) for h in range(N_HEADS_B)]
    kv_cols = [slice(j * HEAD_DIM, (j + 1) * HEAD_DIM) for j in range(N_KV_B)]

    for j in range(N_KV_B):
        qs = jnp.concatenate([q_refs[j][:, gc] for gc in group_cols], axis=0)
        g2 = lax.dot_general(kmean_scr[j], qs, NT_DIMS, preferred_element_type=F32)
        gate = g2[:nblk] + g2[nblk:]
        beaten = jnp.zeros((nblk, qrows), F32)
        for i in range(nblk):
            gi = gate[i:i + 1, :]
            beats = (gi > gate) | ((gi == gate) & (blk > i))
            beaten = beaten + jnp.where(beats, jnp.where(i < n, 1.0, 0.0), 0.0)
        bias_t = jnp.where((blk < n) & (beaten < MOBA_TOPK), 0.0, NEG_INF)

        for g in range(MOBA_GROUP):
            h = j * MOBA_GROUP + g
            bias_scr[h] = bias_t[:, g * blk_sz:(g + 1) * blk_sz]
            qext_scr[h, :, 0:HEAD_DIM] = q_refs[j][:, group_cols[g]]

    def attend(i, own):
        st = pl.multiple_of(i * blk_sz, blk_sz)
        if own:
            kfeat = own_feat
        else:
            far = ((n - i) * blk_sz).astype(F32)
            kfeat = jnp.where(far_lanes, -far, jnp.where(off_lanes, key_off, 0.0)).astype(BF16)
        logits = []
        for j in range(N_KV_B):
            k_ext = jnp.concatenate([k_ref[pl.ds(st, blk_sz), kv_cols[j]], kfeat], axis=1)
            for g in range(MOBA_GROUP):
                logits.append(lax.dot_general(k_ext, qext_scr[j * MOBA_GROUP + g], NT_DIMS,
                                              preferred_element_type=F32))
        for h in range(N_HEADS_B):
            t = logits[h] * to_log2
            if own:
                t = t + causal
                m_new = jnp.max(t, axis=0, keepdims=True)
                p = jnp.exp2(t - m_new)
            else:
                bias = bias_scr[h, pl.ds(i, 1), :]
                m_old = m_scr[h][0:1, :]
                m_new = jnp.maximum(m_old, jnp.max(logits[h], axis=0, keepdims=True) * to_log2 + bias)
                p = jnp.exp2(t + (bias - m_new))
            l_new = jnp.sum(p, axis=0, keepdims=True)
            acc = jnp.dot(vt_ref[i, kv_cols[h // MOBA_GROUP], :], p.astype(BF16), preferred_element_type=F32)
            if not own:
                alpha = jnp.exp2(m_old - m_new)
                l_new = alpha * l_scr[h][0:1, :] + l_new
                acc = alpha * acc_scr[h] + acc
            m_scr[h] = jnp.broadcast_to(m_new, (8, blk_sz))
            l_scr[h] = jnp.broadcast_to(l_new, (8, blk_sz))
            acc_scr[h] = acc

    attend(n, own=True)

    def past_block(i, carry):
        attend(i, own=False)
        return carry

    lax.fori_loop(0, n, past_block, 0)

    for h in range(N_HEADS_B):
        y_scr[:, head_cols[h]] = (acc_scr[h] / l_scr[h][0:1, :]).T
    o_ref[...] = _rms(y_scr[...], g_ref[...]).astype(o_ref.dtype)


def _moba(proj, slopes, norm_b):
    bsz, s, _ = proj.shape
    nblk = s // MOBA_BLOCK
    v_t = proj[:, :, OFF_VB:OFF_VB + KV_B].reshape(bsz, nblk, MOBA_BLOCK, KV_B).transpose(0, 1, 3, 2)
    q_width = MOBA_GROUP * HEAD_DIM

    def q_block(j, b, n):
        return (b, n, OFF_QB // q_width + j)

    return pl.pallas_call(
        _moba_kernel,
        grid=(bsz, nblk),
        in_specs=[pl.BlockSpec(memory_space=pltpu.SMEM),
                  *[pl.BlockSpec((None, MOBA_BLOCK, q_width), functools.partial(q_block, j))
                    for j in range(N_KV_B)],
                  pl.BlockSpec((None, s, KV_B), lambda b, n: (b, 0, OFF_KB // KV_B)),
                  pl.BlockSpec((None, nblk, KV_B, MOBA_BLOCK), lambda b, n: (b, 0, 0, 0)),
                  pl.BlockSpec((1, WIDTH_B), lambda b, n: (0, 0))],
        out_specs=pl.BlockSpec((None, MOBA_BLOCK, WIDTH_B), lambda b, n: (b, n, 0)),
        out_shape=jax.ShapeDtypeStruct((bsz, s, WIDTH_B), BF16),
        scratch_shapes=[pltpu.VMEM((N_KV_B, 2 * nblk, HEAD_DIM), BF16),
                        pltpu.VMEM((N_HEADS_B, MOBA_BLOCK, 2 * HEAD_DIM), BF16),
                        pltpu.VMEM((N_HEADS_B, nblk, MOBA_BLOCK), F32),
                        pltpu.VMEM((N_HEADS_B, 8, MOBA_BLOCK), F32),
                        pltpu.VMEM((N_HEADS_B, 8, MOBA_BLOCK), F32),
                        pltpu.VMEM((N_HEADS_B, HEAD_DIM, MOBA_BLOCK), F32),
                        pltpu.VMEM((MOBA_BLOCK, WIDTH_B), F32)],
        compiler_params=_params(
            2, [((MOBA_BLOCK, WIDTH_B), BF16)] * 2 + [((s, KV_B), BF16)] * 2 + [((1, WIDTH_B), F32)],
            held=[((N_HEADS_B, MOBA_BLOCK, 2 * HEAD_DIM), BF16), ((N_HEADS_B, nblk + 16, MOBA_BLOCK), F32),
                  ((N_HEADS_B, HEAD_DIM, MOBA_BLOCK), F32), ((MOBA_BLOCK, WIDTH_B), F32)]
            + [((N_HEADS_B, MOBA_BLOCK, MOBA_BLOCK), F32)] * 2 + [((MOBA_BLOCK, WIDTH_B), F32)]),
        name="moba",
    )(slopes, *([proj] * N_KV_B), proj, v_t, norm_b.reshape(1, WIDTH_B))


def _alibi_slopes(n):
    return 2.0 ** (-8.0 * jnp.arange(1, n + 1, dtype=F32) / n)


def _ffn(h, w_gate, w_up, w_down):
    act = _gate_up(h, w_gate, w_up)
    return _matmul([act], w_down.astype(BF16), FFN_DOWN_TILE, name="ffn_down")


def kernel(x, c, w_ada, b_ada, ffn1_norm_pre, ffn1_norm_post, ffn1_w_gate, ffn1_w_up, ffn1_w_down, mix_norm_pre, mix_norm_post, w_in, attn_sinks, norm_a, norm_b, w_out, ffn2_norm_pre, ffn2_norm_post, ffn2_w_gate, ffn2_w_up, ffn2_w_down):
    bsz, s, d = x.shape
    depth = w_ada.shape[0]
    slopes_a = _alibi_slopes(N_HEADS_A)
    slopes_b = _alibi_slopes(N_HEADS_B)
    for l in range(depth):
        mod = _ada(c, w_ada[l], b_ada[l]).reshape(bsz, N_MOD, 1, d)
        shift, scale, gate = (lambda i: mod[:, 3 * i]), (lambda i: mod[:, 3 * i + 1]), (lambda i: mod[:, 3 * i + 2])

        h = _norm_mod(x, pre=(ffn1_norm_pre[l], scale(0), shift(0)))[0]
        y = _ffn(h.reshape(bsz * s, d), ffn1_w_gate[l], ffn1_w_up[l], ffn1_w_down[l])

        x, h = _norm_mod(x, post=(y.reshape(bsz, s, d), ffn1_norm_post[l], gate(0)),
                         pre=(mix_norm_pre[l], scale(1), shift(1)), res_w=0.5)
        proj = _matmul([h.reshape(bsz * s, d)], w_in[l].astype(BF16), PROJ_TILE, name="w_in")
        proj = proj.reshape(bsz, s, D_IN)
        ya = _swa(proj, attn_sinks[l], slopes_a, norm_a[l])
        yb = _moba(proj, slopes_b, norm_b[l])
        y = _matmul([ya.reshape(bsz * s, WIDTH_A), yb.reshape(bsz * s, WIDTH_B)], w_out[l].astype(BF16),
                    PROJ_TILE, name="w_out")

        x, h = _norm_mod(x, post=(y.reshape(bsz, s, d), mix_norm_post[l], gate(1)),
                         pre=(ffn2_norm_pre[l], scale(2), shift(2)), res_w=1.0)
        y = _ffn(h.reshape(bsz * s, d), ffn2_w_gate[l], ffn2_w_up[l], ffn2_w_down[l])
        x = _norm_mod(x, post=(y.reshape(bsz, s, d), ffn2_norm_post[l], gate(2)), res_w=0.5)[0]
    return x
```
